```python
import math
import jax, jax.numpy as jnp
from jax import lax
import numpy as np

D_MODEL = 1024
BATCH = 8
SEQ = 2048
DEPTH = 1

LRU_WIDTH = D_MODEL
LRU_BLOCKS = 16
LRU_BLOCK_W = LRU_WIDTH // LRU_BLOCKS
CONV_WIDTH = 4
LRU_C = 8.0
HEAD_DIM = 64
N_Q_HEADS = 16
N_KV_HEADS = 2
GQA_GROUP = N_Q_HEADS // N_KV_HEADS
ATTN_WIDTH = N_Q_HEADS * HEAD_DIM
KV_WIDTH = N_KV_HEADS * HEAD_DIM
WINDOW = 128
BLOCK = 128
IN_WIDTH = 2 * LRU_WIDTH + ATTN_WIDTH + 2 * KV_WIDTH
MIX_WIDTH = LRU_WIDTH + ATTN_WIDTH
D_FF = 4 * D_MODEL
EPS = 1e-6
NEG_INF = -1e30

kernel_name = "hymba_rglru_swa_sink_hybrid"


def rmsnorm(x, g):
    x32 = x.astype(jnp.float32)
    y = x32 * lax.rsqrt(jnp.mean(x32 * x32, axis=-1, keepdims=True) + EPS)
    return (y * g.astype(jnp.float32)).astype(x.dtype)


def causal_depthwise_conv(x, w, b):
    t = x.shape[1]
    xp = jnp.pad(x, ((0, 0), (CONV_WIDTH - 1, 0), (0, 0)))
    y = sum(w[k] * xp[:, k:k + t] for k in range(CONV_WIDTH))
    return y + b


def rg_lru(x, w_a, b_a, w_x, b_x, lam):
    bsz, t, _ = x.shape
    xb = x.reshape(bsz, t, LRU_BLOCKS, LRU_BLOCK_W)
    gate_r = jax.nn.sigmoid(jnp.einsum('btnc,ncd->btnd', xb, w_a).reshape(bsz, t, LRU_WIDTH) + b_a)
    gate_i = jax.nn.sigmoid(jnp.einsum('btnc,ncd->btnd', xb, w_x).reshape(bsz, t, LRU_WIDTH) + b_x)
    r32 = gate_r.astype(jnp.float32)
    log_a = -LRU_C * r32 * jax.nn.softplus(-lam.astype(jnp.float32))
    a = jnp.exp(log_a)
    mult = jnp.sqrt(-jnp.expm1(2.0 * log_a))
    bterm = mult * (gate_i * x).astype(jnp.float32)

    def combine(lhs, rhs):
        a_l, b_l = lhs
        a_r, b_r = rhs
        return a_l * a_r, a_r * b_l + b_r

    _, h = lax.associative_scan(combine, (a, bterm), axis=1)
    return h.astype(x.dtype)


def sliding_window_sink_attention(q, k, v, sinks):
    bsz, t, _ = q.shape
    nb = t // BLOCK
    qb = q.reshape(bsz, nb, BLOCK, N_KV_HEADS, GQA_GROUP, HEAD_DIM)
    kb = k.reshape(bsz, nb, BLOCK, N_KV_HEADS, HEAD_DIM)
    vb = v.reshape(bsz, nb, BLOCK, N_KV_HEADS, HEAD_DIM)
    zero_blk = jnp.zeros_like(kb[:, :1])
    kp = jnp.concatenate([zero_blk, kb], axis=1)
    vp = jnp.concatenate([zero_blk, vb], axis=1)
    kwin = jnp.concatenate([kp[:, :-1], kp[:, 1:]], axis=2)
    vwin = jnp.concatenate([vp[:, :-1], vp[:, 1:]], axis=2)

    scale = 1.0 / math.sqrt(HEAD_DIM)
    scores = jnp.einsum('bnqhgd,bnkhd->bnhgqk', qb, kwin).astype(jnp.float32) * scale
    blk = jnp.arange(nb)[:, None, None]
    qpos = blk * BLOCK + jnp.arange(BLOCK)[None, :, None]
    kpos = (blk - 1) * BLOCK + jnp.arange(2 * BLOCK)[None, None, :]
    mask = (kpos <= qpos) & (kpos > qpos - WINDOW) & (kpos >= 0)
    scores = jnp.where(mask[None, :, None, None], scores, NEG_INF)

    sink = sinks.astype(jnp.float32).reshape(1, 1, N_KV_HEADS, GQA_GROUP, 1, 1)
    m = jnp.maximum(jnp.max(scores, axis=-1, keepdims=True), sink)
    p = jnp.exp(scores - m)
    denom = jnp.sum(p, axis=-1, keepdims=True) + jnp.exp(sink - m)
    probs = (p / denom).astype(v.dtype)
    out = jnp.einsum('bnhgqk,bnkhd->bnqhgd', probs, vwin)
    return out.reshape(bsz, t, ATTN_WIDTH)


def setup_inputs(seed: int = 0) -> dict:
    key = jax.random.key(seed)
    ks = jax.random.split(key, 24)
    f32 = jnp.float32

    def nrm(k, shape, scale):
        return jax.random.normal(k, shape, f32) * scale

    def gain(k, n):
        return jnp.ones((DEPTH, n), f32) + 0.02 * jax.random.normal(k, (DEPTH, n), f32)

    x = jax.random.normal(ks[0], (BATCH, SEQ, D_MODEL), f32)
    base = jax.random.uniform(ks[9], (DEPTH, LRU_WIDTH), f32, minval=0.9, maxval=0.999)
    s = base ** (1.0 / LRU_C)
    lru_lambda = jnp.log(s) - jnp.log1p(-s)
    return {
        "x": x,
        "norm_mix_g": gain(ks[1], D_MODEL),
        "w_in": nrm(ks[2], (DEPTH, D_MODEL, IN_WIDTH), D_MODEL ** -0.5),
        "conv_w": nrm(ks[3], (DEPTH, CONV_WIDTH, LRU_WIDTH), CONV_WIDTH ** -0.5),
        "conv_b": nrm(ks[4], (DEPTH, LRU_WIDTH), 0.02),
        "w_gate_a": nrm(ks[5], (DEPTH, LRU_BLOCKS, LRU_BLOCK_W, LRU_BLOCK_W), LRU_BLOCK_W ** -0.5),
        "b_gate_a": nrm(ks[6], (DEPTH, LRU_WIDTH), 0.02),
        "w_gate_x": nrm(ks[7], (DEPTH, LRU_BLOCKS, LRU_BLOCK_W, LRU_BLOCK_W), LRU_BLOCK_W ** -0.5),
        "b_gate_x": nrm(ks[8], (DEPTH, LRU_WIDTH), 0.02),
        "lru_lambda": lru_lambda,
        "attn_sinks": nrm(ks[10], (DEPTH, N_Q_HEADS), 0.5),
        "lru_out_g": gain(ks[11], LRU_WIDTH),
        "attn_out_g": gain(ks[12], ATTN_WIDTH),
        "w_out": nrm(ks[13], (DEPTH, MIX_WIDTH, D_MODEL), MIX_WIDTH ** -0.5),
        "norm_mlp_g": gain(ks[14], D_MODEL),
        "w_mlp_up": nrm(ks[15], (DEPTH, D_MODEL, D_FF), D_MODEL ** -0.5),
        "w_mlp_down": nrm(ks[16], (DEPTH, D_FF, D_MODEL), D_FF ** -0.5),
        "norm_final_g": jnp.ones((D_MODEL,), f32) + 0.02 * jax.random.normal(ks[17], (D_MODEL,), f32),
    }


def reference(x, norm_mix_g, w_in, conv_w, conv_b, w_gate_a, b_gate_a, w_gate_x, b_gate_x,
              lru_lambda, attn_sinks, lru_out_g, attn_out_g, w_out, norm_mlp_g,
              w_mlp_up, w_mlp_down, norm_final_g):
    s1 = LRU_WIDTH
    s2 = 2 * LRU_WIDTH
    s3 = s2 + ATTN_WIDTH
    s4 = s3 + KV_WIDTH
    for l in range(DEPTH):
        hn = rmsnorm(x, norm_mix_g[l])
        proj = jnp.einsum('btd,de->bte', hn, w_in[l])
        x_lru = proj[..., :s1]
        g_lru = proj[..., s1:s2]
        q = proj[..., s2:s3]
        k = proj[..., s3:s4]
        v = proj[..., s4:]

        xc = causal_depthwise_conv(x_lru, conv_w[l], conv_b[l])
        h = rg_lru(xc, w_gate_a[l], b_gate_a[l], w_gate_x[l], b_gate_x[l], lru_lambda[l])
        y_lru = h * jax.nn.gelu(g_lru, approximate=True)

        y_attn = sliding_window_sink_attention(q, k, v, attn_sinks[l])

        mixed = jnp.concatenate([rmsnorm(y_lru, lru_out_g[l]),
                                 rmsnorm(y_attn, attn_out_g[l])], axis=-1)
        x = x + jnp.einsum('bte,ed->btd', mixed, w_out[l])

        hm = rmsnorm(x, norm_mlp_g[l])
        up = jnp.einsum('btd,df->btf', hm, w_mlp_up[l])
        x = x + jnp.einsum('btf,fd->btd', jnp.square(jax.nn.relu(up)), w_mlp_down[l])
    return rmsnorm(x, norm_final_g)
```

```python
import functools
import math

import jax
import jax.numpy as jnp
from jax import lax
from jax.experimental import pallas as pl
from jax.experimental.pallas import tpu as pltpu

F32 = jnp.float32
BF16 = jnp.bfloat16

EPS = 1e-6
NEG_INF = -1e30
LRU_C = 8.0
CONV_WIDTH = 4
HEAD_DIM = 64
N_Q_HEADS = 16
N_KV_HEADS = 2
ATTN_BLOCK = 128
LANES = 128
GATE_CHUNK = 256

TM_PROJ = 512
TL_LRU = 256
VMEM_LIMIT = 56 * 1024 * 1024


def _rms(x, g):
    ms = jnp.mean(x * x, axis=-1, keepdims=True)
    return x * lax.rsqrt(ms + EPS) * g


def _in_proj_kernel(x_ref, g_ref, w_ref, xl_ref, gl_ref, q_ref, kv_ref, *, d_lru, d_attn):
    hn = _rms(x_ref[...], g_ref[...]).astype(BF16)
    s1, s2, s3 = d_lru, 2 * d_lru, 2 * d_lru + d_attn
    xl_ref[...] = jnp.dot(hn, w_ref[:, :s1], preferred_element_type=F32)
    gl_ref[...] = jnp.dot(hn, w_ref[:, s1:s2], preferred_element_type=F32).astype(BF16)
    q_ref[...] = jnp.dot(hn, w_ref[:, s2:s3], preferred_element_type=F32).astype(BF16)
    kv_ref[...] = jnp.dot(hn, w_ref[:, s3:], preferred_element_type=F32)


def _in_proj(x2, g, w_in_bf, d_lru, d_attn, d_kv2):
    n, d = x2.shape
    e = w_in_bf.shape[1]
    tm = TM_PROJ
    const = lambda i: (0, 0)
    row = lambda i: (i, 0)
    return pl.pallas_call(
        functools.partial(_in_proj_kernel, d_lru=d_lru, d_attn=d_attn),
        grid=(n // tm,),
        in_specs=[
            pl.BlockSpec((tm, d), row),
            pl.BlockSpec((1, d), const),
            pl.BlockSpec((d, e), const, pipeline_mode=pl.Buffered(1)),
        ],
        out_specs=[
            pl.BlockSpec((tm, d_lru), row),
            pl.BlockSpec((tm, d_lru), row),
            pl.BlockSpec((tm, d_attn), row),
            pl.BlockSpec((tm, d_kv2), row),
        ],
        out_shape=[
            jax.ShapeDtypeStruct((n, d_lru), F32),
            jax.ShapeDtypeStruct((n, d_lru), BF16),
            jax.ShapeDtypeStruct((n, d_attn), BF16),
            jax.ShapeDtypeStruct((n, d_kv2), F32),
        ],
        compiler_params=pltpu.CompilerParams(
            dimension_semantics=("arbitrary",), vmem_limit_bytes=VMEM_LIMIT),
        name="in_proj",
    )(x2, g, w_in_bf)


def _shift_rows(a, s, fill, row_idx):
    n, w = a.shape
    if s % 8 == 0:
        return jnp.concatenate([jnp.full((s, w), fill, a.dtype), a[: n - s]], axis=0)
    return jnp.where(row_idx >= s, pltpu.roll(a, s, axis=0), fill)


def _lru_kernel(xl_ref, gl_ref, cw_ref, cb_ref, wg_ref, ba_ref, bx_ref, lam_ref, og_ref,
                o_ref, xpad_ref, hc_ref, y_ref):
    tl, d = xl_ref.shape
    t = pl.program_id(1)

    @pl.when(t == 0)
    def _():
        xpad_ref[0:8, :] = jnp.zeros((8, d), F32)
        hc_ref[...] = jnp.zeros_like(hc_ref)

    xpad_ref[8:tl + 8, :] = xl_ref[...]
    row_idx = lax.broadcasted_iota(jnp.int32, (tl, GATE_CHUNK), 0)
    ssq = jnp.zeros((tl, 1), F32)
    for c in range(d // GATE_CHUNK):
        cs = slice(c * GATE_CHUNK, (c + 1) * GATE_CHUNK)
        xc = cb_ref[:, cs] + cw_ref[3:4, cs] * xpad_ref[8:tl + 8, cs]
        for k in range(CONV_WIDTH - 1):
            off = 8 - (CONV_WIDTH - 1) + k
            xc = xc + cw_ref[k:k + 1, cs] * xpad_ref[off:off + tl, cs]
        z = jnp.dot(xc.astype(BF16), wg_ref[c], preferred_element_type=F32)
        gate_r = jax.nn.sigmoid(z[:, :GATE_CHUNK] + ba_ref[:, cs])
        gate_i = jax.nn.sigmoid(z[:, GATE_CHUNK:] + bx_ref[:, cs])
        log_a = (-LRU_C) * gate_r * jax.nn.softplus(-lam_ref[:, cs])
        a = jnp.exp(log_a)
        mult = jnp.sqrt(1.0 - a * a)
        b = mult * (gate_i * xc)
        s = 1
        while s < tl:
            a_s = _shift_rows(a, s, 1.0, row_idx)
            b_s = _shift_rows(b, s, 0.0, row_idx)
            b = a * b_s + b
            a = a * a_s
            s *= 2
        h = a * hc_ref[0:1, cs] + b
        hc_ref[0:1, cs] = h[tl - 1:tl, :]
        y = h * jax.nn.gelu(gl_ref[:, cs].astype(F32), approximate=True)
        y_ref[:, cs] = y
        ssq = ssq + jnp.sum(y * y, axis=-1, keepdims=True)
    xpad_ref[0:8, :] = xpad_ref[tl:tl + 8, :]
    scale = lax.rsqrt(ssq * (1.0 / d) + EPS)
    o_ref[...] = (y_ref[...] * scale * og_ref[...]).astype(BF16)


def _lru(xl3, gl3, conv_w, conv_b, wg, b_a, b_x, lam, out_g):
    bsz, t, d = xl3.shape
    tl = TL_LRU
    tile = lambda b, i: (b, i, 0)
    c2 = lambda b, i: (0, 0)
    c3 = lambda b, i: (0, 0, 0)
    vec = pl.BlockSpec((1, d), c2)
    return pl.pallas_call(
        _lru_kernel,
        grid=(bsz, t // tl),
        in_specs=[
            pl.BlockSpec((None, tl, d), tile),
            pl.BlockSpec((None, tl, d), tile),
            pl.BlockSpec((CONV_WIDTH, d), c2),
            vec,
            pl.BlockSpec(wg.shape, c3),
            vec, vec, vec, vec,
        ],
        out_specs=pl.BlockSpec((None, tl, d), tile),
        out_shape=jax.ShapeDtypeStruct((bsz, t, d), BF16),
        scratch_shapes=[
            pltpu.VMEM((tl + 8, d), F32),
            pltpu.VMEM((8, d), F32),
            pltpu.VMEM((tl, d), F32),
        ],
        compiler_params=pltpu.CompilerParams(
            dimension_semantics=("arbitrary", "arbitrary"), vmem_limit_bytes=VMEM_LIMIT),
        name="lru",
    )(xl3, gl3, conv_w, conv_b, wg, b_a, b_x, lam, out_g)


def _attn_kernel(sink_ref, q_ref, kv_ref, og_ref, o_ref, kvprev_ref, y_ref):
    blk, d_attn = q_ref.shape
    j = pl.program_id(1)
    heads_per_tile = LANES // HEAD_DIM
    tiles_per_kv = d_attn // N_KV_HEADS // LANES
    m_rows = tiles_per_kv * blk

    @pl.when(j == 0)
    def _():
        kvprev_ref[...] = jnp.zeros_like(kvprev_ref)

    kv_cur = kv_ref[...]
    win = jnp.concatenate([kvprev_ref[...], kv_cur], axis=0)
    kvprev_ref[...] = kv_cur
    kwin, vwin = win[:, :LANES], win[:, LANES:]
    kwin_r = pltpu.roll(kwin, HEAD_DIM, axis=1)
    vwin_r = pltpu.roll(vwin, HEAD_DIM, axis=1)
    lane = lax.broadcasted_iota(jnp.int32, (2 * blk, LANES), 1)
    lo = lane < HEAD_DIM

    def half(x_same, x_rolled, kv_head, parity):
        src = x_same if kv_head == parity else x_rolled
        keep = lo if parity == 0 else jnp.logical_not(lo)
        return jnp.where(keep, src, 0.0).astype(BF16)

    qi = lax.broadcasted_iota(jnp.int32, (m_rows, 2 * blk), 0) % blk
    kj = lax.broadcasted_iota(jnp.int32, (m_rows, 2 * blk), 1)
    first_key = jnp.where(j > 0, 0, blk)
    valid = (kj > qi) & (kj <= qi + blk) & (kj >= first_key)

    scale = 1.0 / math.sqrt(HEAD_DIM)
    for h in range(N_KV_HEADS):
        q_st = jnp.concatenate(
            [q_ref[:, (h * tiles_per_kv + c) * LANES:(h * tiles_per_kv + c + 1) * LANES]
             for c in range(tiles_per_kv)], axis=0)
        q_st = (q_st.astype(F32) * scale).astype(BF16)
        acc = None
        for p in range(heads_per_tile):
            k_hp = half(kwin, kwin_r, h, p)
            v_hp = half(vwin, vwin_r, h, p)
            s = lax.dot_general(q_st, k_hp, (((1,), (1,)), ((), ())),
                                preferred_element_type=F32)
            s = jnp.where(valid, s, NEG_INF)
            sink = jnp.concatenate(
                [jnp.full((blk, 1), sink_ref[h * tiles_per_kv * heads_per_tile + c * heads_per_tile + p], F32)
                 for c in range(tiles_per_kv)], axis=0)
            m = jnp.maximum(jnp.max(s, axis=-1, keepdims=True), sink)
            e = jnp.exp(s - m)
            denom = jnp.sum(e, axis=-1, keepdims=True) + jnp.exp(sink - m)
            probs = (e / denom).astype(BF16)
            o = jnp.dot(probs, v_hp, preferred_element_type=F32)
            acc = o if acc is None else acc + o
        for c in range(tiles_per_kv):
            col = (h * tiles_per_kv + c) * LANES
            y_ref[:, col:col + LANES] = acc[c * blk:(c + 1) * blk, :]
    y = y_ref[...]
    o_ref[...] = _rms(y, og_ref[...]).astype(BF16)


def _attn(sinks, q3, kv3, out_g):
    bsz, t, d_attn = q3.shape
    blk = ATTN_BLOCK
    tile = lambda b, j: (b, j, 0)
    return pl.pallas_call(
        _attn_kernel,
        grid=(bsz, t // blk),
        in_specs=[
            pl.BlockSpec(memory_space=pltpu.SMEM),
            pl.BlockSpec((None, blk, d_attn), tile),
            pl.BlockSpec((None, blk, kv3.shape[2]), tile),
            pl.BlockSpec((1, d_attn), lambda b, j: (0, 0)),
        ],
        out_specs=pl.BlockSpec((None, blk, d_attn), tile),
        out_shape=jax.ShapeDtypeStruct((bsz, t, d_attn), BF16),
        scratch_shapes=[
            pltpu.VMEM((blk, kv3.shape[2]), F32),
            pltpu.VMEM((blk, d_attn), F32),
        ],
        compiler_params=pltpu.CompilerParams(
            dimension_semantics=("arbitrary", "arbitrary"), vmem_limit_bytes=VMEM_LIMIT),
        name="attn",
    )(sinks, q3, kv3, out_g)


def _out_mlp_kernel(x_ref, yl_ref, ya_ref, wo_ref, gm_ref, wu_ref, wd_ref, gf_ref, o_ref, *,
                    d_lru, ff_chunk, final_norm):
    mix = jnp.dot(yl_ref[...], wo_ref[:d_lru, :], preferred_element_type=F32)
    mix = mix + jnp.dot(ya_ref[...], wo_ref[d_lru:, :], preferred_element_type=F32)
    x1 = x_ref[...] + mix
    hm = _rms(x1, gm_ref[...]).astype(BF16)
    mlp = None
    for c in range(wu_ref.shape[1] // ff_chunk):
        cs = slice(c * ff_chunk, (c + 1) * ff_chunk)
        up = jnp.dot(hm, wu_ref[:, cs], preferred_element_type=F32)
        act = jnp.square(jnp.maximum(up, 0.0)).astype(BF16)
        down = jnp.dot(act, wd_ref[cs, :], preferred_element_type=F32)
        mlp = down if mlp is None else mlp + down
    x2 = x1 + mlp
    o_ref[...] = _rms(x2, gf_ref[...]) if final_norm else x2


def _out_mlp(x2, yl2, ya2, w_out_bf, g_mlp, w_up_bf, w_down_bf, g_final, final_norm):
    n, d = x2.shape
    d_lru, d_attn = yl2.shape[1], ya2.shape[1]
    d_ff = w_up_bf.shape[1]
    tm = TM_PROJ
    row = lambda i: (i, 0)
    const = lambda i: (0, 0)
    wspec = lambda shape: pl.BlockSpec(shape, const, pipeline_mode=pl.Buffered(1))
    return pl.pallas_call(
        functools.partial(_out_mlp_kernel, d_lru=d_lru, ff_chunk=1024, final_norm=final_norm),
        grid=(n // tm,),
        in_specs=[
            pl.BlockSpec((tm, d), row),
            pl.BlockSpec((tm, d_lru), row),
            pl.BlockSpec((tm, d_attn), row),
            wspec((d_lru + d_attn, d)),
            pl.BlockSpec((1, d), const),
            wspec((d, d_ff)),
            wspec((d_ff, d)),
            pl.BlockSpec((1, d), const),
        ],
        out_specs=pl.BlockSpec((tm, d), row),
        out_shape=jax.ShapeDtypeStruct((n, d), F32),
        compiler_params=pltpu.CompilerParams(
            dimension_semantics=("arbitrary",), vmem_limit_bytes=VMEM_LIMIT),
        name="out_mlp",
    )(x2, yl2, ya2, w_out_bf, g_mlp, w_up_bf, w_down_bf, g_final)


def _pack_gates(w_a, w_x):
    def bd(w):
        nblk, bw, _ = w.shape
        per = GATE_CHUNK // bw
        w4 = w.reshape(nblk // per, per, bw, bw)
        eye = jnp.eye(per, dtype=w.dtype)
        return jnp.einsum('cikl,ij->cikjl', w4, eye).reshape(nblk // per, GATE_CHUNK, GATE_CHUNK)
    return jnp.concatenate([bd(w_a), bd(w_x)], axis=-1).astype(BF16)


def kernel(x, norm_mix_g, w_in, conv_w, conv_b, w_gate_a, b_gate_a, w_gate_x, b_gate_x, lru_lambda,
           attn_sinks, lru_out_g, attn_out_g, w_out, norm_mlp_g, w_mlp_up, w_mlp_down, norm_final_g):
    bsz, t, d = x.shape
    depth = w_in.shape[0]
    d_lru = conv_w.shape[2]
    d_attn = attn_out_g.shape[1]
    d_kv2 = w_in.shape[2] - 2 * d_lru - d_attn
    assert d_kv2 == 2 * LANES and d_attn == N_Q_HEADS * HEAD_DIM and d_lru % GATE_CHUNK == 0
    assert t % TL_LRU == 0 and t % ATTN_BLOCK == 0 and (bsz * t) % TM_PROJ == 0
    n = bsz * t
    x2 = x.reshape(n, d)
    r1 = lambda v: v.reshape(1, -1)
    for l in range(depth):
        xl, gl, q, kv = _in_proj(x2, r1(norm_mix_g[l]), w_in[l].astype(BF16), d_lru, d_attn, d_kv2)
        yl = _lru(xl.reshape(bsz, t, d_lru), gl.reshape(bsz, t, d_lru), conv_w[l], r1(conv_b[l]),
                  _pack_gates(w_gate_a[l], w_gate_x[l]), r1(b_gate_a[l]), r1(b_gate_x[l]),
                  r1(lru_lambda[l]), r1(lru_out_g[l]))
        ya = _attn(attn_sinks[l], q.reshape(bsz, t, d_attn), kv.reshape(bsz, t, d_kv2), r1(attn_out_g[l]))
        x2 = _out_mlp(x2, yl.reshape(n, d_lru), ya.reshape(n, d_attn), w_out[l].astype(BF16),
                      r1(norm_mlp_g[l]), w_mlp_up[l].astype(BF16), w_mlp_down[l].astype(BF16),
                      r1(norm_final_g), final_norm=(l == depth - 1))
    return x2.reshape(bsz, t, d)
```

```python
import functools
import math

import jax
import jax.numpy as jnp
from jax import lax
from jax.experimental import pallas as pl
from jax.experimental.pallas import tpu as pltpu

F32 = jnp.float32
BF16 = jnp.bfloat16

EPS = 1e-6
NEG_INF = -1e30
LRU_C = 8.0
CONV_WIDTH = 4
HEAD_DIM = 64
N_Q_HEADS = 16
N_KV_HEADS = 2
ATTN_BLOCK = 128
LANES = 128
GATE_CHUNK = 256

TM_PROJ = 512
TL_LRU = 256
TQ_ATTN = 512
VMEM_LIMIT = 56 * 1024 * 1024


def _rms(x, g):
    ms = jnp.mean(x * x, axis=-1, keepdims=True)
    return x * lax.rsqrt(ms + EPS) * g


def _in_proj_kernel(x_ref, g_ref, w_ref, xl_ref, gl_ref, q_ref, kv_ref, *, d_lru, d_attn):
    hn = _rms(x_ref[...], g_ref[...]).astype(BF16)
    s1, s2, s3 = d_lru, 2 * d_lru, 2 * d_lru + d_attn
    xl_ref[...] = jnp.dot(hn, w_ref[:, :s1], preferred_element_type=F32)
    gl_ref[...] = jnp.dot(hn, w_ref[:, s1:s2], preferred_element_type=F32).astype(BF16)
    q_ref[...] = jnp.dot(hn, w_ref[:, s2:s3], preferred_element_type=F32).astype(BF16)
    kv_ref[...] = jnp.dot(hn, w_ref[:, s3:], preferred_element_type=F32)


def _in_proj(x2, g, w_in_bf, d_lru, d_attn, d_kv2):
    n, d = x2.shape
    e = w_in_bf.shape[1]
    tm = TM_PROJ
    const = lambda i: (0, 0)
    row = lambda i: (i, 0)
    return pl.pallas_call(
        functools.partial(_in_proj_kernel, d_lru=d_lru, d_attn=d_attn),
        grid=(n // tm,),
        in_specs=[
            pl.BlockSpec((tm, d), row),
            pl.BlockSpec((1, d), const),
            pl.BlockSpec((d, e), const, pipeline_mode=pl.Buffered(1)),
        ],
        out_specs=[
            pl.BlockSpec((tm, d_lru), row),
            pl.BlockSpec((tm, d_lru), row),
            pl.BlockSpec((tm, d_attn), row),
            pl.BlockSpec((tm, d_kv2), row),
        ],
        out_shape=[
            jax.ShapeDtypeStruct((n, d_lru), F32),
            jax.ShapeDtypeStruct((n, d_lru), BF16),
            jax.ShapeDtypeStruct((n, d_attn), BF16),
            jax.ShapeDtypeStruct((n, d_kv2), F32),
        ],
        compiler_params=pltpu.CompilerParams(
            dimension_semantics=("arbitrary",), vmem_limit_bytes=VMEM_LIMIT),
        name="in_proj",
    )(x2, g, w_in_bf)


def _shift_rows(a, s, fill, row_idx):
    n, w = a.shape
    if s % 8 == 0:
        return jnp.concatenate([jnp.full((s, w), fill, a.dtype), a[: n - s]], axis=0)
    return jnp.where(row_idx >= s, pltpu.roll(a, s, axis=0), fill)


def _lru_kernel(xl_ref, gl_ref, cw_ref, cb_ref, wg_ref, ba_ref, bx_ref, lam_ref, og_ref,
                o_ref, xpad_ref, hc_ref, y_ref):
    tl, d = xl_ref.shape
    t = pl.program_id(1)

    @pl.when(t == 0)
    def _():
        xpad_ref[0:8, :] = jnp.zeros((8, d), F32)
        hc_ref[...] = jnp.zeros_like(hc_ref)

    xpad_ref[8:tl + 8, :] = xl_ref[...]
    row_idx = lax.broadcasted_iota(jnp.int32, (tl, GATE_CHUNK), 0)
    ssq = jnp.zeros((tl, 1), F32)
    for c in range(d // GATE_CHUNK):
        cs = slice(c * GATE_CHUNK, (c + 1) * GATE_CHUNK)
        xc = cb_ref[:, cs] + cw_ref[3:4, cs] * xpad_ref[8:tl + 8, cs]
        for k in range(CONV_WIDTH - 1):
            off = 8 - (CONV_WIDTH - 1) + k
            xc = xc + cw_ref[k:k + 1, cs] * xpad_ref[off:off + tl, cs]
        z = jnp.dot(xc.astype(BF16), wg_ref[c], preferred_element_type=F32)
        gate_r = jax.nn.sigmoid(z[:, :GATE_CHUNK] + ba_ref[:, cs])
        gate_i = jax.nn.sigmoid(z[:, GATE_CHUNK:] + bx_ref[:, cs])
        log_a = (-LRU_C) * gate_r * jax.nn.softplus(-lam_ref[:, cs])
        a = jnp.exp(log_a)
        mult = jnp.sqrt(1.0 - a * a)
        b = mult * (gate_i * xc)
        s = 1
        while s < tl:
            a_s = _shift_rows(a, s, 1.0, row_idx)
            b_s = _shift_rows(b, s, 0.0, row_idx)
            b = a * b_s + b
            a = a * a_s
            s *= 2
        h = a * hc_ref[0:1, cs] + b
        hc_ref[0:1, cs] = h[tl - 1:tl, :]
        y = h * jax.nn.gelu(gl_ref[:, cs].astype(F32), approximate=True)
        y_ref[:, cs] = y
        ssq = ssq + jnp.sum(y * y, axis=-1, keepdims=True)
    xpad_ref[0:8, :] = xpad_ref[tl:tl + 8, :]
    scale = lax.rsqrt(ssq * (1.0 / d) + EPS)
    o_ref[...] = (y_ref[...] * scale * og_ref[...]).astype(BF16)


def _lru(xl3, gl3, conv_w, conv_b, wg, b_a, b_x, lam, out_g):
    bsz, t, d = xl3.shape
    tl = TL_LRU
    tile = lambda b, i: (b, i, 0)
    c2 = lambda b, i: (0, 0)
    c3 = lambda b, i: (0, 0, 0)
    vec = pl.BlockSpec((1, d), c2)
    return pl.pallas_call(
        _lru_kernel,
        grid=(bsz, t // tl),
        in_specs=[
            pl.BlockSpec((None, tl, d), tile),
            pl.BlockSpec((None, tl, d), tile),
            pl.BlockSpec((CONV_WIDTH, d), c2),
            vec,
            pl.BlockSpec(wg.shape, c3),
            vec, vec, vec, vec,
        ],
        out_specs=pl.BlockSpec((None, tl, d), tile),
        out_shape=jax.ShapeDtypeStruct((bsz, t, d), BF16),
        scratch_shapes=[
            pltpu.VMEM((tl + 8, d), F32),
            pltpu.VMEM((8, d), F32),
            pltpu.VMEM((tl, d), F32),
        ],
        compiler_params=pltpu.CompilerParams(
            dimension_semantics=("arbitrary", "arbitrary"), vmem_limit_bytes=VMEM_LIMIT),
        name="lru",
    )(xl3, gl3, conv_w, conv_b, wg, b_a, b_x, lam, out_g)


def _attn_bias(tile_rows, blk):
    qi = lax.broadcasted_iota(jnp.int32, (tile_rows, 2 * blk), 0) % blk
    kj = lax.broadcasted_iota(jnp.int32, (tile_rows, 2 * blk), 1)
    band = (kj > qi) & (kj <= qi + blk)
    return jnp.stack([jnp.where(band, 0.0, NEG_INF), jnp.where(band & (kj >= blk), 0.0, NEG_INF)]).astype(F32)


def _attn_kernel(sink_ref, bias_ref, q_ref, kv_ref, og_ref, o_ref, kvprev_ref, y_ref):
    tq, d_attn = q_ref.shape
    blk = ATTN_BLOCK
    n_blk = tq // blk
    j = pl.program_id(1)
    heads_per_tile = LANES // HEAD_DIM
    tiles_per_kv = d_attn // N_KV_HEADS // LANES

    @pl.when(j == 0)
    def _():
        kvprev_ref[...] = jnp.zeros_like(kvprev_ref)

    kv_cur = kv_ref[...]
    kv_all = jnp.concatenate([kvprev_ref[...], kv_cur], axis=0)
    kvprev_ref[...] = kv_cur[tq - blk:, :]
    k_all = kv_all[:, :LANES] * (1.0 / math.sqrt(HEAD_DIM))
    v_all = kv_all[:, LANES:]
    k_rol = pltpu.roll(k_all, HEAD_DIM, axis=1)
    v_rol = pltpu.roll(v_all, HEAD_DIM, axis=1)
    lo = lax.broadcasted_iota(jnp.int32, k_all.shape, 1) < HEAD_DIM
    ones = jnp.ones(k_all.shape, BF16)

    def half(x_same, x_rolled, kv_head, parity):
        src = x_same if kv_head == parity else x_rolled
        keep = lo if parity == 0 else jnp.logical_not(lo)
        return jnp.where(keep, src, 0.0).astype(BF16)

    k_hp = [[half(k_all, k_rol, h, p) for p in range(heads_per_tile)] for h in range(N_KV_HEADS)]
    v_hp = [[jnp.concatenate([half(v_all, v_rol, h, p), ones], axis=1) for p in range(heads_per_tile)]
            for h in range(N_KV_HEADS)]

    for i in range(n_blk):
        rows = slice(i * blk, (i + 1) * blk)
        win = slice(i * blk, (i + 2) * blk)
        bias = bias_ref[jnp.where(j == 0, 1, 0)] if i == 0 else bias_ref[0]
        for h in range(N_KV_HEADS):
            q_st = jnp.concatenate(
                [q_ref[rows, (h * tiles_per_kv + c) * LANES:(h * tiles_per_kv + c + 1) * LANES]
                 for c in range(tiles_per_kv)], axis=0)
            acc = None
            for p in range(heads_per_tile):
                s = lax.dot_general(q_st, k_hp[h][p][win], (((1,), (1,)), ((), ())),
                                    preferred_element_type=F32) + bias
                sink = jnp.concatenate(
                    [jnp.full((blk, LANES), sink_ref[(h * tiles_per_kv + c) * heads_per_tile + p], F32)
                     for c in range(tiles_per_kv)], axis=0)
                m = jnp.maximum(jnp.broadcast_to(jnp.max(s, axis=-1, keepdims=True), sink.shape), sink)
                e = jnp.concatenate([jnp.exp(s[:, :blk] - m), jnp.exp(s[:, blk:] - m)], axis=1).astype(BF16)
                ov = jnp.dot(e, v_hp[h][p][win], preferred_element_type=F32)
                denom = ov[:, LANES:] + jnp.exp(sink - m)
                o = ov[:, :LANES] / denom
                acc = o if acc is None else acc + o
            for c in range(tiles_per_kv):
                col = (h * tiles_per_kv + c) * LANES
                y_ref[rows, col:col + LANES] = acc[c * blk:(c + 1) * blk, :]
    o_ref[...] = _rms(y_ref[...], og_ref[...]).astype(BF16)


def _attn(sinks, q3, kv3, out_g):
    bsz, t, d_attn = q3.shape
    tq = TQ_ATTN
    bias = _attn_bias(d_attn // N_KV_HEADS // LANES * ATTN_BLOCK, ATTN_BLOCK)
    tile = lambda b, j: (b, j, 0)
    return pl.pallas_call(
        _attn_kernel,
        grid=(bsz, t // tq),
        in_specs=[
            pl.BlockSpec(memory_space=pltpu.SMEM),
            pl.BlockSpec(bias.shape, lambda b, j: (0, 0, 0), pipeline_mode=pl.Buffered(1)),
            pl.BlockSpec((None, tq, d_attn), tile),
            pl.BlockSpec((None, tq, kv3.shape[2]), tile),
            pl.BlockSpec((1, d_attn), lambda b, j: (0, 0)),
        ],
        out_specs=pl.BlockSpec((None, tq, d_attn), tile),
        out_shape=jax.ShapeDtypeStruct((bsz, t, d_attn), BF16),
        scratch_shapes=[
            pltpu.VMEM((ATTN_BLOCK, kv3.shape[2]), F32),
            pltpu.VMEM((tq, d_attn), F32),
        ],
        compiler_params=pltpu.CompilerParams(
            dimension_semantics=("arbitrary", "arbitrary"), vmem_limit_bytes=VMEM_LIMIT),
        name="attn",
    )(sinks, bias, q3, kv3, out_g)


def _out_mlp_kernel(x_ref, yl_ref, ya_ref, wo_ref, gm_ref, wu_ref, wd_ref, gf_ref, o_ref, *,
                    d_lru, ff_chunk, final_norm):
    mix = jnp.dot(yl_ref[...], wo_ref[:d_lru, :], preferred_element_type=F32)
    mix = mix + jnp.dot(ya_ref[...], wo_ref[d_lru:, :], preferred_element_type=F32)
    x1 = x_ref[...] + mix
    hm = _rms(x1, gm_ref[...]).astype(BF16)
    mlp = None
    for c in range(wu_ref.shape[1] // ff_chunk):
        cs = slice(c * ff_chunk, (c + 1) * ff_chunk)
        up = jnp.dot(hm, wu_ref[:, cs], preferred_element_type=F32)
        act = jnp.square(jnp.maximum(up, 0.0)).astype(BF16)
        down = jnp.dot(act, wd_ref[cs, :], preferred_element_type=F32)
        mlp = down if mlp is None else mlp + down
    x2 = x1 + mlp
    o_ref[...] = _rms(x2, gf_ref[...]) if final_norm else x2


def _out_mlp(x2, yl2, ya2, w_out_bf, g_mlp, w_up_bf, w_down_bf, g_final, final_norm):
    n, d = x2.shape
    d_lru, d_attn = yl2.shape[1], ya2.shape[1]
    d_ff = w_up_bf.shape[1]
    tm = TM_PROJ
    row = lambda i: (i, 0)
    const = lambda i: (0, 0)
    wspec = lambda shape: pl.BlockSpec(shape, const, pipeline_mode=pl.Buffered(1))
    return pl.pallas_call(
        functools.partial(_out_mlp_kernel, d_lru=d_lru, ff_chunk=1024, final_norm=final_norm),
        grid=(n // tm,),
        in_specs=[
            pl.BlockSpec((tm, d), row),
            pl.BlockSpec((tm, d_lru), row),
            pl.BlockSpec((tm, d_attn), row),
            wspec((d_lru + d_attn, d)),
            pl.BlockSpec((1, d), const),
            wspec((d, d_ff)),
            wspec((d_ff, d)),
            pl.BlockSpec((1, d), const),
        ],
        out_specs=pl.BlockSpec((tm, d), row),
        out_shape=jax.ShapeDtypeStruct((n, d), F32),
        compiler_params=pltpu.CompilerParams(
            dimension_semantics=("arbitrary",), vmem_limit_bytes=VMEM_LIMIT),
        name="out_mlp",
    )(x2, yl2, ya2, w_out_bf, g_mlp, w_up_bf, w_down_bf, g_final)


def _pack_gates(w_a, w_x):
    def bd(w):
        nblk, bw, _ = w.shape
        per = GATE_CHUNK // bw
        w4 = w.reshape(nblk // per, per, bw, bw)
        eye = jnp.eye(per, dtype=w.dtype)
        return jnp.einsum('cikl,ij->cikjl', w4, eye).reshape(nblk // per, GATE_CHUNK, GATE_CHUNK)
    return jnp.concatenate([bd(w_a), bd(w_x)], axis=-1).astype(BF16)


def kernel(x, norm_mix_g, w_in, conv_w, conv_b, w_gate_a, b_gate_a, w_gate_x, b_gate_x, lru_lambda,
           attn_sinks, lru_out_g, attn_out_g, w_out, norm_mlp_g, w_mlp_up, w_mlp_down, norm_final_g):
    bsz, t, d = x.shape
    depth = w_in.shape[0]
    d_lru = conv_w.shape[2]
    d_attn = attn_out_g.shape[1]
    d_kv2 = w_in.shape[2] - 2 * d_lru - d_attn
    assert d_kv2 == 2 * LANES and d_attn == N_Q_HEADS * HEAD_DIM and d_lru % GATE_CHUNK == 0
    assert t % TL_LRU == 0 and t % TQ_ATTN == 0 and TQ_ATTN % ATTN_BLOCK == 0 and (bsz * t) % TM_PROJ == 0
    n = bsz * t
    x2 = x.reshape(n, d)
    r1 = lambda v: v.reshape(1, -1)
    for l in range(depth):
        xl, gl, q, kv = _in_proj(x2, r1(norm_mix_g[l]), w_in[l].astype(BF16), d_lru, d_attn, d_kv2)
        yl = _lru(xl.reshape(bsz, t, d_lru), gl.reshape(bsz, t, d_lru), conv_w[l], r1(conv_b[l]),
                  _pack_gates(w_gate_a[l], w_gate_x[l]), r1(b_gate_a[l]), r1(b_gate_x[l]),
                  r1(lru_lambda[l]), r1(lru_out_g[l]))
        ya = _attn(attn_sinks[l], q.reshape(bsz, t, d_attn), kv.reshape(bsz, t, d_kv2), r1(attn_out_g[l]))
        x2 = _out_mlp(x2, yl.reshape(n, d_lru), ya.reshape(n, d_attn), w_out[l].astype(BF16),
                      r1(norm_mlp_g[l]), w_mlp_up[l].astype(BF16), w_mlp_down[l].astype(BF16),
                      r1(norm_final_g), final_norm=(l == depth - 1))
    return x2.reshape(bsz, t, d)
```

```python
import functools
import math

import jax
import jax.numpy as jnp
from jax import lax
from jax.experimental import pallas as pl
from jax.experimental.pallas import tpu as pltpu

F32 = jnp.float32
BF16 = jnp.bfloat16

EPS = 1e-6
NEG_INF = -1e30
LRU_C = 8.0
CONV_WIDTH = 4
HEAD_DIM = 64
N_Q_HEADS = 16
N_KV_HEADS = 2
ATTN_BLOCK = 128
LANES = 128
SUBLANES = 8
GATE_CHUNK = 256

TM_PROJ = 512
TL_LRU = 256
TQ_ATTN = 512
VMEM_LIMIT = 56 * 1024 * 1024


def _rms(x, g):
    ms = jnp.mean(x * x, axis=-1, keepdims=True)
    return x * lax.rsqrt(ms + EPS) * g


def _in_proj_kernel(x_ref, g_ref, w_ref, xl_ref, gl_ref, q_ref, kv_ref, *, d_lru, d_attn):
    nb, tt, d = x_ref.shape
    hn = _rms(x_ref[...].reshape(nb * tt, d), g_ref[...]).astype(BF16)
    s1, s2, s3 = d_lru, 2 * d_lru, 2 * d_lru + d_attn
    for dst, lo_col in ((xl_ref, 0), (gl_ref, s1)):
        y = jnp.dot(hn, w_ref[:, lo_col:lo_col + d_lru], preferred_element_type=F32)
        for b in range(nb):
            for s in range(d_lru // LANES):
                dst[s, pl.ds(b, tt, stride=nb), :] = y[b * tt:(b + 1) * tt, s * LANES:(s + 1) * LANES]
    q = jnp.dot(hn, w_ref[:, s2:s3], preferred_element_type=F32).astype(BF16)
    q_ref[...] = q.reshape(nb, tt, d_attn)
    kv = jnp.dot(hn, w_ref[:, s3:], preferred_element_type=F32)
    kv_ref[...] = kv.reshape(nb, tt, kv.shape[1])


def _in_proj(x, g, w_in_bf, d_lru, d_attn, d_kv2):
    bsz, t, d = x.shape
    e = w_in_bf.shape[1]
    tt = TM_PROJ // bsz
    n_slab = d_lru // LANES
    const = lambda i: (0, 0)
    tile = lambda i: (0, i, 0)
    slab_spec = pl.BlockSpec((n_slab, tt * bsz, LANES), tile)
    slab_shape = jax.ShapeDtypeStruct((n_slab, t * bsz, LANES), F32)
    return pl.pallas_call(
        functools.partial(_in_proj_kernel, d_lru=d_lru, d_attn=d_attn),
        grid=(t // tt,),
        in_specs=[
            pl.BlockSpec((bsz, tt, d), tile),
            pl.BlockSpec((1, d), const),
            pl.BlockSpec((d, e), const, pipeline_mode=pl.Buffered(1)),
        ],
        out_specs=[
            slab_spec,
            slab_spec,
            pl.BlockSpec((bsz, tt, d_attn), tile),
            pl.BlockSpec((bsz, tt, d_kv2), tile),
        ],
        out_shape=[
            slab_shape,
            slab_shape,
            jax.ShapeDtypeStruct((bsz, t, d_attn), BF16),
            jax.ShapeDtypeStruct((bsz, t, d_kv2), F32),
        ],
        compiler_params=pltpu.CompilerParams(
            dimension_semantics=("arbitrary",), vmem_limit_bytes=VMEM_LIMIT),
        name="in_proj",
    )(x, g, w_in_bf)


def _slab_pair(ref, c, rows=slice(None)):
    return jnp.concatenate([ref[2 * c, rows, :], ref[2 * c + 1, rows, :]], axis=1)


def _lru_kernel(xl_ref, gl_ref, cw_ref, cb_ref, wg_ref, ba_ref, bx_ref, lam_ref, og_ref,
                o_ref, tail_ref, hc_ref, y_ref):
    n_slab, r, _ = xl_ref.shape
    nb = o_ref.shape[1]
    tt = r // nb
    halo = (CONV_WIDTH - 1) * nb
    d = n_slab * LANES

    @pl.when(pl.program_id(0) == 0)
    def _():
        tail_ref[...] = jnp.zeros_like(tail_ref)
        hc_ref[...] = jnp.zeros_like(hc_ref)

    ssq = jnp.zeros((r, LANES), F32)
    for c in range(n_slab // 2):
        cs = slice(c * GATE_CHUNK, (c + 1) * GATE_CHUNK)
        x = _slab_pair(xl_ref, c)
        xe = jnp.concatenate([_slab_pair(tail_ref, c), x], axis=0)
        tail_ref[2 * c] = xl_ref[2 * c, r - halo:r, :]
        tail_ref[2 * c + 1] = xl_ref[2 * c + 1, r - halo:r, :]
        xc = cb_ref[:, cs] + cw_ref[CONV_WIDTH - 1:CONV_WIDTH, cs] * x
        for k in range(CONV_WIDTH - 1):
            xc = xc + cw_ref[k:k + 1, cs] * xe[k * nb:k * nb + r]
        z = jnp.dot(xc.astype(BF16), wg_ref[c], preferred_element_type=F32)
        gate_r = jax.nn.sigmoid(z[:, :GATE_CHUNK] + ba_ref[:, cs])
        gate_i = jax.nn.sigmoid(z[:, GATE_CHUNK:] + bx_ref[:, cs])
        log_a = gate_r * ((-LRU_C) * jax.nn.softplus(-lam_ref[:, cs]))
        a = jnp.exp(log_a)
        b = jnp.sqrt(1.0 - a * a) * (gate_i * xc)
        h = _slab_pair(hc_ref, c)
        hs = []
        for t in range(tt):
            h = a[t * nb:(t + 1) * nb] * h + b[t * nb:(t + 1) * nb]
            hs.append(h)
        hc_ref[2 * c] = h[:, :LANES]
        hc_ref[2 * c + 1] = h[:, LANES:]
        y = jnp.concatenate(hs, axis=0) * jax.nn.gelu(_slab_pair(gl_ref, c), approximate=True)
        y_ref[2 * c] = y[:, :LANES]
        y_ref[2 * c + 1] = y[:, LANES:]
        ysq = y * y
        ssq = ssq + ysq[:, :LANES] + ysq[:, LANES:]
    scale = lax.rsqrt(jnp.sum(ssq, axis=-1, keepdims=True) * (1.0 / d) + EPS)
    for s in range(n_slab):
        y_ref[s] = y_ref[s] * scale * og_ref[:, s * LANES:(s + 1) * LANES]
    for s in range(n_slab):
        for b in range(nb):
            o_ref[s, b] = y_ref[s, pl.ds(b, tt, stride=nb), :]


def _lru(xl_s, gl_s, bsz, conv_w, conv_b, wg, b_a, b_x, lam, out_g):
    n_slab, rows, _ = xl_s.shape
    t = rows // bsz
    d = n_slab * LANES
    tt = TL_LRU // bsz
    r = tt * bsz
    tile = lambda i: (0, i, 0)
    c2 = lambda i: (0, 0)
    vec = pl.BlockSpec((1, d), c2)
    return pl.pallas_call(
        _lru_kernel,
        grid=(t // tt,),
        in_specs=[
            pl.BlockSpec((n_slab, r, LANES), tile),
            pl.BlockSpec((n_slab, r, LANES), tile),
            pl.BlockSpec((CONV_WIDTH, d), c2),
            vec,
            pl.BlockSpec(wg.shape, lambda i: (0, 0, 0)),
            vec, vec, vec, vec,
        ],
        out_specs=pl.BlockSpec((n_slab, bsz, tt, LANES), lambda i: (0, 0, i, 0)),
        out_shape=jax.ShapeDtypeStruct((n_slab, bsz, t, LANES), F32),
        scratch_shapes=[
            pltpu.VMEM((n_slab, (CONV_WIDTH - 1) * bsz, LANES), F32),
            pltpu.VMEM((n_slab, bsz, LANES), F32),
            pltpu.VMEM((n_slab, r, LANES), F32),
        ],
        compiler_params=pltpu.CompilerParams(
            dimension_semantics=("arbitrary",), vmem_limit_bytes=VMEM_LIMIT),
        name="lru",
    )(xl_s, gl_s, conv_w, conv_b, wg, b_a, b_x, lam, out_g)


def _attn_bias(tile_rows, blk):
    qi = lax.broadcasted_iota(jnp.int32, (tile_rows, 2 * blk), 0) % blk
    kj = lax.broadcasted_iota(jnp.int32, (tile_rows, 2 * blk), 1)
    band = (kj > qi) & (kj <= qi + blk)
    return jnp.stack([jnp.where(band, 0.0, NEG_INF), jnp.where(band & (kj >= blk), 0.0, NEG_INF)]).astype(F32)


def _attn_kernel(sink_ref, bias_ref, q_ref, kv_ref, og_ref, o_ref, kvprev_ref, y_ref):
    tq, d_attn = q_ref.shape
    blk = ATTN_BLOCK
    n_blk = tq // blk
    j = pl.program_id(1)
    heads_per_tile = LANES // HEAD_DIM
    tiles_per_kv = d_attn // N_KV_HEADS // LANES

    @pl.when(j == 0)
    def _():
        kvprev_ref[...] = jnp.zeros_like(kvprev_ref)

    kv_cur = kv_ref[...]
    kv_all = jnp.concatenate([kvprev_ref[...], kv_cur], axis=0)
    kvprev_ref[...] = kv_cur[tq - blk:, :]
    k_all = kv_all[:, :LANES] * (1.0 / math.sqrt(HEAD_DIM))
    v_all = kv_all[:, LANES:]
    k_rol = pltpu.roll(k_all, HEAD_DIM, axis=1)
    v_rol = pltpu.roll(v_all, HEAD_DIM, axis=1)
    lo = lax.broadcasted_iota(jnp.int32, k_all.shape, 1) < HEAD_DIM
    ones = jnp.ones(k_all.shape, BF16)

    def half(x_same, x_rolled, kv_head, parity):
        src = x_same if kv_head == parity else x_rolled
        keep = lo if parity == 0 else jnp.logical_not(lo)
        return jnp.where(keep, src, 0.0).astype(BF16)

    k_hp = [[half(k_all, k_rol, h, p) for p in range(heads_per_tile)] for h in range(N_KV_HEADS)]
    v_hp = [[jnp.concatenate([half(v_all, v_rol, h, p), ones], axis=1) for p in range(heads_per_tile)]
            for h in range(N_KV_HEADS)]

    for i in range(n_blk):
        rows = slice(i * blk, (i + 1) * blk)
        win = slice(i * blk, (i + 2) * blk)
        bias = bias_ref[jnp.where(j == 0, 1, 0)] if i == 0 else bias_ref[0]
        for h in range(N_KV_HEADS):
            q_st = jnp.concatenate(
                [q_ref[rows, (h * tiles_per_kv + c) * LANES:(h * tiles_per_kv + c + 1) * LANES]
                 for c in range(tiles_per_kv)], axis=0)
            acc = None
            for p in range(heads_per_tile):
                s = lax.dot_general(q_st, k_hp[h][p][win], (((1,), (1,)), ((), ())),
                                    preferred_element_type=F32) + bias
                sink = jnp.concatenate(
                    [jnp.full((blk, LANES), sink_ref[(h * tiles_per_kv + c) * heads_per_tile + p], F32)
                     for c in range(tiles_per_kv)], axis=0)
                m = jnp.maximum(jnp.broadcast_to(jnp.max(s, axis=-1, keepdims=True), sink.shape), sink)
                e = jnp.concatenate([jnp.exp(s[:, :blk] - m), jnp.exp(s[:, blk:] - m)], axis=1).astype(BF16)
                ov = jnp.dot(e, v_hp[h][p][win], preferred_element_type=F32)
                denom = ov[:, LANES:] + jnp.exp(sink - m)
                o = ov[:, :LANES] / denom
                acc = o if acc is None else acc + o
            for c in range(tiles_per_kv):
                col = (h * tiles_per_kv + c) * LANES
                y_ref[rows, col:col + LANES] = acc[c * blk:(c + 1) * blk, :]
    o_ref[...] = _rms(y_ref[...], og_ref[...]).astype(BF16)


def _attn(sinks, q3, kv3, out_g):
    bsz, t, d_attn = q3.shape
    tq = TQ_ATTN
    bias = _attn_bias(d_attn // N_KV_HEADS // LANES * ATTN_BLOCK, ATTN_BLOCK)
    tile = lambda b, j: (b, j, 0)
    return pl.pallas_call(
        _attn_kernel,
        grid=(bsz, t // tq),
        in_specs=[
            pl.BlockSpec(memory_space=pltpu.SMEM),
            pl.BlockSpec(bias.shape, lambda b, j: (0, 0, 0), pipeline_mode=pl.Buffered(1)),
            pl.BlockSpec((None, tq, d_attn), tile),
            pl.BlockSpec((None, tq, kv3.shape[2]), tile),
            pl.BlockSpec((1, d_attn), lambda b, j: (0, 0)),
        ],
        out_specs=pl.BlockSpec((None, tq, d_attn), tile),
        out_shape=jax.ShapeDtypeStruct((bsz, t, d_attn), BF16),
        scratch_shapes=[
            pltpu.VMEM((ATTN_BLOCK, kv3.shape[2]), F32),
            pltpu.VMEM((tq, d_attn), F32),
        ],
        compiler_params=pltpu.CompilerParams(
            dimension_semantics=("arbitrary", "arbitrary"), vmem_limit_bytes=VMEM_LIMIT),
        name="attn",
    )(sinks, bias, q3, kv3, out_g)


def _out_mlp_kernel(x_ref, yl_ref, ya_ref, wo_ref, gm_ref, wu_ref, wd_ref, gf_ref, o_ref, *,
                    d_lru, ff_chunk, final_norm):
    yl = jnp.concatenate([yl_ref[s] for s in range(yl_ref.shape[0])], axis=1).astype(BF16)
    mix = jnp.dot(yl, wo_ref[:d_lru, :], preferred_element_type=F32)
    mix = mix + jnp.dot(ya_ref[...], wo_ref[d_lru:, :], preferred_element_type=F32)
    x1 = x_ref[...] + mix
    hm = _rms(x1, gm_ref[...]).astype(BF16)
    mlp = None
    for c in range(wu_ref.shape[1] // ff_chunk):
        cs = slice(c * ff_chunk, (c + 1) * ff_chunk)
        up = jnp.dot(hm, wu_ref[:, cs], preferred_element_type=F32)
        act = jnp.square(jnp.maximum(up, 0.0)).astype(BF16)
        down = jnp.dot(act, wd_ref[cs, :], preferred_element_type=F32)
        mlp = down if mlp is None else mlp + down
    x2 = x1 + mlp
    o_ref[...] = _rms(x2, gf_ref[...]) if final_norm else x2


def _out_mlp(x2, yl_s, ya2, w_out_bf, g_mlp, w_up_bf, w_down_bf, g_final, final_norm):
    n, d = x2.shape
    n_slab = yl_s.shape[0]
    d_lru, d_attn = n_slab * LANES, ya2.shape[1]
    d_ff = w_up_bf.shape[1]
    tm = TM_PROJ
    row = lambda i: (i, 0)
    const = lambda i: (0, 0)
    wspec = lambda shape: pl.BlockSpec(shape, const, pipeline_mode=pl.Buffered(1))
    return pl.pallas_call(
        functools.partial(_out_mlp_kernel, d_lru=d_lru, ff_chunk=1024, final_norm=final_norm),
        grid=(n // tm,),
        in_specs=[
            pl.BlockSpec((tm, d), row),
            pl.BlockSpec((n_slab, tm, LANES), lambda i: (0, i, 0)),
            pl.BlockSpec((tm, d_attn), row),
            wspec((d_lru + d_attn, d)),
            pl.BlockSpec((1, d), const),
            wspec((d, d_ff)),
            wspec((d_ff, d)),
            pl.BlockSpec((1, d), const),
        ],
        out_specs=pl.BlockSpec((tm, d), row),
        out_shape=jax.ShapeDtypeStruct((n, d), F32),
        compiler_params=pltpu.CompilerParams(
            dimension_semantics=("arbitrary",), vmem_limit_bytes=VMEM_LIMIT),
        name="out_mlp",
    )(x2, yl_s, ya2, w_out_bf, g_mlp, w_up_bf, w_down_bf, g_final)


def _pack_gates(w_a, w_x):
    def bd(w):
        nblk, bw, _ = w.shape
        per = GATE_CHUNK // bw
        w4 = w.reshape(nblk // per, per, bw, bw)
        eye = jnp.eye(per, dtype=w.dtype)
        return jnp.einsum('cikl,ij->cikjl', w4, eye).reshape(nblk // per, GATE_CHUNK, GATE_CHUNK)
    return jnp.concatenate([bd(w_a), bd(w_x)], axis=-1).astype(BF16)


def kernel(x, norm_mix_g, w_in, conv_w, conv_b, w_gate_a, b_gate_a, w_gate_x, b_gate_x, lru_lambda,
           attn_sinks, lru_out_g, attn_out_g, w_out, norm_mlp_g, w_mlp_up, w_mlp_down, norm_final_g):
    bsz, t, d = x.shape
    depth = w_in.shape[0]
    d_lru = conv_w.shape[2]
    d_attn = attn_out_g.shape[1]
    d_kv2 = w_in.shape[2] - 2 * d_lru - d_attn
    assert d_kv2 == 2 * LANES and d_attn == N_Q_HEADS * HEAD_DIM and d_lru % GATE_CHUNK == 0
    assert bsz == SUBLANES and ATTN_BLOCK == LANES
    assert t % (TL_LRU // bsz) == 0 and t % (TM_PROJ // bsz) == 0 and (bsz * t) % TM_PROJ == 0
    assert t % TQ_ATTN == 0 and TQ_ATTN % ATTN_BLOCK == 0
    n = bsz * t
    x2 = x.reshape(n, d)
    r1 = lambda v: v.reshape(1, -1)
    for l in range(depth):
        xl_s, gl_s, q, kv = _in_proj(x2.reshape(bsz, t, d), r1(norm_mix_g[l]), w_in[l].astype(BF16),
                                     d_lru, d_attn, d_kv2)
        yl_s = _lru(xl_s, gl_s, bsz, conv_w[l], r1(conv_b[l]),
                    _pack_gates(w_gate_a[l], w_gate_x[l]), r1(b_gate_a[l]), r1(b_gate_x[l]),
                    r1(lru_lambda[l]), r1(lru_out_g[l]))
        ya = _attn(attn_sinks[l], q, kv, r1(attn_out_g[l]))
        x2 = _out_mlp(x2, yl_s.reshape(d_lru // LANES, n, LANES), ya.reshape(n, d_attn), w_out[l].astype(BF16),
                      r1(norm_mlp_g[l]), w_mlp_up[l].astype(BF16), w_mlp_down[l].astype(BF16),
                      r1(norm_final_g), final_norm=(l == depth - 1))
    return x2.reshape(bsz, t, d)
```

```python
import functools
import math

import jax
import jax.numpy as jnp
from jax import lax
from jax.experimental import pallas as pl
from jax.experimental.pallas import tpu as pltpu

F32 = jnp.float32
BF16 = jnp.bfloat16

EPS = 1e-6
NEG_INF = -1e30
LRU_C = 8.0
CONV_WIDTH = 4
HEAD_DIM = 64
N_Q_HEADS = 16
N_KV_HEADS = 2
ATTN_BLOCK = 128
LANES = 128
SUBLANES = 8
GATE_CHUNK = 256

TM_PROJ = 512
TQ_ATTN = 512
VMEM_LIMIT = 56 * 1024 * 1024


def _rms(x, g):
    ms = jnp.mean(x * x, axis=-1, keepdims=True)
    return x * lax.rsqrt(ms + EPS) * g


def _in_lru_kernel(x_ref, g_ref, w_ref, cw_ref, cb_ref, wg_ref, ba_ref, bx_ref, lam_ref, og_ref,
                   yl_ref, q_ref, kv_ref, hn_ref, tail_ref, hc_ref, y_ref, *, d_lru, d_attn):
    nb, tt, d = x_ref.shape
    r = nb * tt
    halo = (CONV_WIDTH - 1) * nb
    s1, s2, s3 = d_lru, 2 * d_lru, 2 * d_lru + d_attn

    @pl.when(pl.program_id(0) == 0)
    def _():
        tail_ref[...] = jnp.zeros_like(tail_ref)
        hc_ref[...] = jnp.zeros_like(hc_ref)

    for b in range(nb):
        hb = _rms(x_ref[b], g_ref[...])
        for s in range(d // LANES):
            hn_ref[s, pl.ds(b, tt, stride=nb), :] = hb[:, s * LANES:(s + 1) * LANES]
    hn = jnp.concatenate([hn_ref[s] for s in range(d // LANES)], axis=1).astype(BF16)

    def project(lo_col, width=GATE_CHUNK):
        return jnp.dot(hn, w_ref[:, lo_col:lo_col + width], preferred_element_type=F32)

    n_chunks = d_lru // GATE_CHUNK
    q_cols = d_attn // n_chunks
    ssq = jnp.zeros((r, 1), F32)
    nxt = (project(0), project(s1))
    for c in range(n_chunks):
        cs = slice(c * GATE_CHUNK, (c + 1) * GATE_CHUNK)
        x, gate = nxt
        xe = jnp.concatenate([tail_ref[:, cs], x], axis=0)
        tail_ref[:, cs] = x[r - halo:, :]
        xc = cb_ref[:, cs] + cw_ref[CONV_WIDTH - 1:CONV_WIDTH, cs] * x
        for k in range(CONV_WIDTH - 1):
            xc = xc + cw_ref[k:k + 1, cs] * xe[k * nb:k * nb + r]
        z = jnp.dot(xc.astype(BF16), wg_ref[c], preferred_element_type=F32)
        if c + 1 < n_chunks:
            nxt = (project((c + 1) * GATE_CHUNK), project(s1 + (c + 1) * GATE_CHUNK))
        q_ref[:, c * q_cols:(c + 1) * q_cols] = project(s2 + c * q_cols, q_cols).astype(BF16)
        gate_r = jax.nn.sigmoid(z[:, :GATE_CHUNK] + ba_ref[:, cs])
        gate_i = jax.nn.sigmoid(z[:, GATE_CHUNK:] + bx_ref[:, cs])
        log_a = gate_r * ((-LRU_C) * jax.nn.softplus(-lam_ref[:, cs]))
        a = jnp.exp(log_a)
        om = 1.0 - a * a
        bterm = jnp.where(om > 0.0, om * lax.rsqrt(om), 0.0) * (gate_i * xc)
        h = hc_ref[:, cs]
        hs = []
        for t in range(tt):
            h = a[t * nb:(t + 1) * nb] * h + bterm[t * nb:(t + 1) * nb]
            hs.append(h)
        hc_ref[:, cs] = h
        y = jnp.concatenate(hs, axis=0) * jax.nn.gelu(gate, approximate=True)
        y_ref[:, cs] = y
        ssq = ssq + jnp.sum(y * y, axis=-1, keepdims=True)
    kv_ref[...] = project(s3, kv_ref.shape[1])
    scale = lax.rsqrt(ssq * (1.0 / d_lru) + EPS)
    yl_ref[...] = (y_ref[...] * scale * og_ref[...]).astype(BF16)


def _in_lru(x, g, w_in_bf, conv_w, conv_b, wg, b_a, b_x, lam, out_g, d_attn, d_kv2):
    bsz, t, d = x.shape
    e = w_in_bf.shape[1]
    d_lru = conv_w.shape[1]
    tt = TM_PROJ // bsz
    r = tt * bsz
    c2 = lambda i: (0, 0)
    row = lambda i: (i, 0)
    vec = pl.BlockSpec((1, d_lru), c2)
    return pl.pallas_call(
        functools.partial(_in_lru_kernel, d_lru=d_lru, d_attn=d_attn),
        grid=(t // tt,),
        in_specs=[
            pl.BlockSpec((bsz, tt, d), lambda i: (0, i, 0)),
            pl.BlockSpec((1, d), c2),
            pl.BlockSpec((d, e), c2, pipeline_mode=pl.Buffered(1)),
            pl.BlockSpec((CONV_WIDTH, d_lru), c2),
            vec,
            pl.BlockSpec(wg.shape, lambda i: (0, 0, 0)),
            vec, vec, vec, vec,
        ],
        out_specs=[
            pl.BlockSpec((r, d_lru), row),
            pl.BlockSpec((r, d_attn), row),
            pl.BlockSpec((r, d_kv2), row),
        ],
        out_shape=[
            jax.ShapeDtypeStruct((t * bsz, d_lru), BF16),
            jax.ShapeDtypeStruct((t * bsz, d_attn), BF16),
            jax.ShapeDtypeStruct((t * bsz, d_kv2), F32),
        ],
        scratch_shapes=[
            pltpu.VMEM((d // LANES, r, LANES), F32),
            pltpu.VMEM(((CONV_WIDTH - 1) * bsz, d_lru), F32),
            pltpu.VMEM((bsz, d_lru), F32),
            pltpu.VMEM((r, d_lru), F32),
        ],
        compiler_params=pltpu.CompilerParams(
            dimension_semantics=("arbitrary",), vmem_limit_bytes=VMEM_LIMIT),
        name="in_lru",
    )(x, g, w_in_bf, conv_w, conv_b, wg, b_a, b_x, lam, out_g)


def _attn_bias(tile_rows, blk):
    qi = lax.broadcasted_iota(jnp.int32, (tile_rows, 2 * blk), 0) % blk
    kj = lax.broadcasted_iota(jnp.int32, (tile_rows, 2 * blk), 1)
    band = (kj > qi) & (kj <= qi + blk)
    return jnp.stack([jnp.where(band, 0.0, NEG_INF), jnp.where(band & (kj >= blk), 0.0, NEG_INF)]).astype(F32)


def _attn_kernel(sink_ref, bias_ref, q_ref, kv_ref, og_ref, o_ref, kvprev_ref, y_ref):
    tq, d_attn = q_ref.shape
    blk = ATTN_BLOCK
    n_blk = tq // blk
    j = pl.program_id(1)
    heads_per_tile = LANES // HEAD_DIM
    tiles_per_kv = d_attn // N_KV_HEADS // LANES

    @pl.when(j == 0)
    def _():
        kvprev_ref[...] = jnp.zeros_like(kvprev_ref)

    kv_cur = kv_ref[...]
    kv_all = jnp.concatenate([kvprev_ref[...], kv_cur], axis=0)
    kvprev_ref[...] = kv_cur[tq - blk:, :]
    k_all = kv_all[:, :LANES] * (1.0 / math.sqrt(HEAD_DIM))
    v_all = kv_all[:, LANES:]
    k_rol = pltpu.roll(k_all, HEAD_DIM, axis=1)
    v_rol = pltpu.roll(v_all, HEAD_DIM, axis=1)
    lo = lax.broadcasted_iota(jnp.int32, k_all.shape, 1) < HEAD_DIM
    ones = jnp.ones(k_all.shape, BF16)

    def half(x_same, x_rolled, kv_head, parity):
        src = x_same if kv_head == parity else x_rolled
        keep = lo if parity == 0 else jnp.logical_not(lo)
        return jnp.where(keep, src, 0.0).astype(BF16)

    k_hp = [[half(k_all, k_rol, h, p) for p in range(heads_per_tile)] for h in range(N_KV_HEADS)]
    v_hp = [[jnp.concatenate([half(v_all, v_rol, h, p), ones], axis=1) for p in range(heads_per_tile)]
            for h in range(N_KV_HEADS)]

    for i in range(n_blk):
        rows = slice(i * blk, (i + 1) * blk)
        win = slice(i * blk, (i + 2) * blk)
        bias = bias_ref[jnp.where(j == 0, 1, 0)] if i == 0 else bias_ref[0]
        for h in range(N_KV_HEADS):
            q_st = jnp.concatenate(
                [q_ref[rows, (h * tiles_per_kv + c) * LANES:(h * tiles_per_kv + c + 1) * LANES]
                 for c in range(tiles_per_kv)], axis=0)
            acc = None
            for p in range(heads_per_tile):
                s = lax.dot_general(q_st, k_hp[h][p][win], (((1,), (1,)), ((), ())),
                                    preferred_element_type=F32) + bias
                sink = jnp.concatenate(
                    [jnp.full((blk, LANES), sink_ref[(h * tiles_per_kv + c) * heads_per_tile + p], F32)
                     for c in range(tiles_per_kv)], axis=0)
                m = jnp.maximum(jnp.broadcast_to(jnp.max(s, axis=-1, keepdims=True), sink.shape), sink)
                e = jnp.concatenate([jnp.exp(s[:, :blk] - m), jnp.exp(s[:, blk:] - m)], axis=1).astype(BF16)
                ov = jnp.dot(e, v_hp[h][p][win], preferred_element_type=F32)
                denom = ov[:, LANES:] + jnp.exp(sink - m)
                o = ov[:, :LANES] / denom
                acc = o if acc is None else acc + o
            for c in range(tiles_per_kv):
                col = (h * tiles_per_kv + c) * LANES
                y_ref[rows, col:col + LANES] = acc[c * blk:(c + 1) * blk, :]
    o_ref[...] = _rms(y_ref[...], og_ref[...]).astype(BF16)


def _attn(sinks, q_tb, kv_tb, bsz, out_g):
    t = q_tb.shape[0]
    d_attn, d_kv2 = q_tb.shape[1] // bsz, kv_tb.shape[1] // bsz
    tq = TQ_ATTN
    bias = _attn_bias(d_attn // N_KV_HEADS // LANES * ATTN_BLOCK, ATTN_BLOCK)
    return pl.pallas_call(
        _attn_kernel,
        grid=(bsz, t // tq),
        in_specs=[
            pl.BlockSpec(memory_space=pltpu.SMEM),
            pl.BlockSpec(bias.shape, lambda b, j: (0, 0, 0), pipeline_mode=pl.Buffered(1)),
            pl.BlockSpec((tq, d_attn), lambda b, j: (j, b)),
            pl.BlockSpec((tq, d_kv2), lambda b, j: (j, b)),
            pl.BlockSpec((1, d_attn), lambda b, j: (0, 0)),
        ],
        out_specs=pl.BlockSpec((None, tq, d_attn), lambda b, j: (b, j, 0)),
        out_shape=jax.ShapeDtypeStruct((bsz, t, d_attn), BF16),
        scratch_shapes=[
            pltpu.VMEM((ATTN_BLOCK, d_kv2), F32),
            pltpu.VMEM((tq, d_attn), F32),
        ],
        compiler_params=pltpu.CompilerParams(
            dimension_semantics=("arbitrary", "arbitrary"), vmem_limit_bytes=VMEM_LIMIT),
        name="attn",
    )(sinks, bias, q_tb, kv_tb, out_g)


def _out_mlp_kernel(x_ref, yl_ref, ya_ref, wo_ref, gm_ref, wu_ref, wd_ref, gf_ref, o_ref, *,
                    d_lru, ff_chunk, final_norm):
    mix = jnp.dot(yl_ref[...], wo_ref[:d_lru, :], preferred_element_type=F32)
    mix = mix + jnp.dot(ya_ref[...], wo_ref[d_lru:, :], preferred_element_type=F32)
    x1 = x_ref[...] + mix
    hm = _rms(x1, gm_ref[...]).astype(BF16)
    mlp = None
    for c in range(wu_ref.shape[1] // ff_chunk):
        cs = slice(c * ff_chunk, (c + 1) * ff_chunk)
        up = jnp.dot(hm, wu_ref[:, cs], preferred_element_type=F32)
        act = jnp.square(jnp.maximum(up, 0.0)).astype(BF16)
        down = jnp.dot(act, wd_ref[cs, :], preferred_element_type=F32)
        mlp = down if mlp is None else mlp + down
    x2 = x1 + mlp
    o_ref[...] = _rms(x2, gf_ref[...]) if final_norm else x2


def _out_mlp(x2, yl_tb, ya2, bsz, w_out_bf, g_mlp, w_up_bf, w_down_bf, g_final, final_norm):
    n, d = x2.shape
    d_lru, d_attn = yl_tb.shape[1] // bsz, ya2.shape[1]
    d_ff = w_up_bf.shape[1]
    tm = TM_PROJ
    tiles_per_seq = n // bsz // tm
    row = lambda i: (i, 0)
    const = lambda i: (0, 0)
    wspec = lambda shape: pl.BlockSpec(shape, const, pipeline_mode=pl.Buffered(1))
    return pl.pallas_call(
        functools.partial(_out_mlp_kernel, d_lru=d_lru, ff_chunk=1024, final_norm=final_norm),
        grid=(n // tm,),
        in_specs=[
            pl.BlockSpec((tm, d), row),
            pl.BlockSpec((tm, d_lru), lambda i: (i % tiles_per_seq, i // tiles_per_seq)),
            pl.BlockSpec((tm, d_attn), row),
            wspec((d_lru + d_attn, d)),
            pl.BlockSpec((1, d), const),
            wspec((d, d_ff)),
            wspec((d_ff, d)),
            pl.BlockSpec((1, d), const),
        ],
        out_specs=pl.BlockSpec((tm, d), row),
        out_shape=jax.ShapeDtypeStruct((n, d), F32),
        compiler_params=pltpu.CompilerParams(
            dimension_semantics=("arbitrary",), vmem_limit_bytes=VMEM_LIMIT),
        name="out_mlp",
    )(x2, yl_tb, ya2, w_out_bf, g_mlp, w_up_bf, w_down_bf, g_final)


def _pack_gates(w_a, w_x):
    def bd(w):
        nblk, bw, _ = w.shape
        per = GATE_CHUNK // bw
        w4 = w.reshape(nblk // per, per, bw, bw)
        eye = jnp.eye(per, dtype=w.dtype)
        return jnp.einsum('cikl,ij->cikjl', w4, eye).reshape(nblk // per, GATE_CHUNK, GATE_CHUNK)
    return jnp.concatenate([bd(w_a), bd(w_x)], axis=-1).astype(BF16)


def kernel(x, norm_mix_g, w_in, conv_w, conv_b, w_gate_a, b_gate_a, w_gate_x, b_gate_x, lru_lambda,
           attn_sinks, lru_out_g, attn_out_g, w_out, norm_mlp_g, w_mlp_up, w_mlp_down, norm_final_g):
    bsz, t, d = x.shape
    depth = w_in.shape[0]
    d_lru = conv_w.shape[2]
    d_attn = attn_out_g.shape[1]
    d_kv2 = w_in.shape[2] - 2 * d_lru - d_attn
    assert d_kv2 == 2 * LANES and d_attn == N_Q_HEADS * HEAD_DIM and d_lru % GATE_CHUNK == 0
    assert bsz == SUBLANES and ATTN_BLOCK == LANES
    assert t % (TM_PROJ // bsz) == 0 and t % TM_PROJ == 0
    assert t % TQ_ATTN == 0 and TQ_ATTN % ATTN_BLOCK == 0
    n = bsz * t
    x2 = x.reshape(n, d)
    r1 = lambda v: v.reshape(1, -1)
    for l in range(depth):
        yl, q, kv = _in_lru(x2.reshape(bsz, t, d), r1(norm_mix_g[l]), w_in[l].astype(BF16), conv_w[l],
                            r1(conv_b[l]), _pack_gates(w_gate_a[l], w_gate_x[l]), r1(b_gate_a[l]),
                            r1(b_gate_x[l]), r1(lru_lambda[l]), r1(lru_out_g[l]), d_attn, d_kv2)
        ya = _attn(attn_sinks[l], q.reshape(t, bsz * d_attn), kv.reshape(t, bsz * d_kv2), bsz, r1(attn_out_g[l]))
        x2 = _out_mlp(x2, yl.reshape(t, bsz * d_lru), ya.reshape(n, d_attn), bsz, w_out[l].astype(BF16),
                      r1(norm_mlp_g[l]), w_mlp_up[l].astype(BF16), w_mlp_down[l].astype(BF16),
                      r1(norm_final_g), final_norm=(l == depth - 1))
    return x2.reshape(bsz, t, d)
```

```python
import functools
import math

import jax
import jax.numpy as jnp
from jax import lax
from jax.experimental import pallas as pl
from jax.experimental.pallas import tpu as pltpu

F32 = jnp.float32
BF16 = jnp.bfloat16

EPS = 1e-6
NEG_INF = -1e30
LRU_C = 8.0
CONV_WIDTH = 4
HEAD_DIM = 64
N_Q_HEADS = 16
N_KV_HEADS = 2
ATTN_BLOCK = 128
LANES = 128
SUBLANES = 8
GATE_CHUNK = 256

TM_PROJ = 512
TQ_ATTN = 512
VMEM_LIMIT = 56 * 1024 * 1024


def _rms(x, g):
    ms = jnp.mean(x * x, axis=-1, keepdims=True)
    return x * lax.rsqrt(ms + EPS) * g


def _in_lru_kernel(x_ref, g_ref, w_ref, cw_ref, cb_ref, wg_ref, ba_ref, bx_ref, lam_ref, og_ref,
                   yl_ref, q_ref, kv_ref, hn_ref, tail_ref, hc_ref, y_ref, *, d_lru, d_attn):
    nb, tt, d = x_ref.shape
    r = nb * tt
    halo = (CONV_WIDTH - 1) * nb
    s1, s2, s3 = d_lru, 2 * d_lru, 2 * d_lru + d_attn

    @pl.when(pl.program_id(0) == 0)
    def _():
        tail_ref[...] = jnp.zeros_like(tail_ref)
        hc_ref[...] = jnp.zeros_like(hc_ref)

    hn_bt = []
    for b in range(nb):
        hb = _rms(x_ref[b], g_ref[...])
        hn_bt.append(hb.astype(BF16))
        for s in range(d // LANES):
            hn_ref[s, pl.ds(b, tt, stride=nb), :] = hb[:, s * LANES:(s + 1) * LANES]
    hn_bt = jnp.concatenate(hn_bt, axis=0)
    hn = jnp.concatenate([hn_ref[s] for s in range(d // LANES)], axis=1).astype(BF16)

    def project(lo_col, width=GATE_CHUNK, lhs=None):
        return jnp.dot(hn if lhs is None else lhs, w_ref[:, lo_col:lo_col + width], preferred_element_type=F32)

    n_chunks = d_lru // GATE_CHUNK
    q_cols = d_attn // n_chunks
    ssq = jnp.zeros((r, 1), F32)
    nxt = (project(0), project(s1))
    for c in range(n_chunks):
        cs = slice(c * GATE_CHUNK, (c + 1) * GATE_CHUNK)
        x, gate = nxt
        xe = jnp.concatenate([tail_ref[:, cs], x], axis=0)
        tail_ref[:, cs] = x[r - halo:, :]
        xc = cb_ref[:, cs] + cw_ref[CONV_WIDTH - 1:CONV_WIDTH, cs] * x
        for k in range(CONV_WIDTH - 1):
            xc = xc + cw_ref[k:k + 1, cs] * xe[k * nb:k * nb + r]
        z = jnp.dot(xc.astype(BF16), wg_ref[c], preferred_element_type=F32)
        if c + 1 < n_chunks:
            nxt = (project((c + 1) * GATE_CHUNK), project(s1 + (c + 1) * GATE_CHUNK))
        q_c = project(s2 + c * q_cols, q_cols, hn_bt).astype(BF16)
        q_ref[:, :, c * q_cols:(c + 1) * q_cols] = q_c.reshape(nb, tt, q_cols)
        gate_r = jax.nn.sigmoid(z[:, :GATE_CHUNK] + ba_ref[:, cs])
        gate_i = jax.nn.sigmoid(z[:, GATE_CHUNK:] + bx_ref[:, cs])
        log_a = gate_r * ((-LRU_C) * jax.nn.softplus(-lam_ref[:, cs]))
        a = jnp.exp(log_a)
        om = 1.0 - a * a
        bterm = jnp.where(om > 0.0, om * lax.rsqrt(om), 0.0) * (gate_i * xc)
        h = hc_ref[:, cs]
        hs = []
        for t in range(tt):
            h = a[t * nb:(t + 1) * nb] * h + bterm[t * nb:(t + 1) * nb]
            hs.append(h)
        hc_ref[:, cs] = h
        y = jnp.concatenate(hs, axis=0) * jax.nn.gelu(gate, approximate=True)
        for s in range(GATE_CHUNK // LANES):
            y_ref[c * (GATE_CHUNK // LANES) + s] = y[:, s * LANES:(s + 1) * LANES]
        ssq = ssq + jnp.sum(y * y, axis=-1, keepdims=True)
    kv = project(s3, kv_ref.shape[2], hn_bt)
    kv_ref[...] = kv.reshape(nb, tt, kv.shape[1])
    scale = lax.rsqrt(ssq * (1.0 / d_lru) + EPS)
    n_slab = d_lru // LANES
    for s in range(n_slab):
        y_ref[s] = y_ref[s] * scale * og_ref[:, s * LANES:(s + 1) * LANES]
    for b in range(nb):
        yl_ref[b] = jnp.concatenate(
            [y_ref[s, pl.ds(b, tt, stride=nb), :] for s in range(n_slab)], axis=1).astype(BF16)


def _in_lru(x, g, w_in_bf, conv_w, conv_b, wg, b_a, b_x, lam, out_g, d_attn, d_kv2):
    bsz, t, d = x.shape
    e = w_in_bf.shape[1]
    d_lru = conv_w.shape[1]
    tt = TM_PROJ // bsz
    r = tt * bsz
    c2 = lambda i: (0, 0)
    tile = lambda i: (0, i, 0)
    vec = pl.BlockSpec((1, d_lru), c2)
    return pl.pallas_call(
        functools.partial(_in_lru_kernel, d_lru=d_lru, d_attn=d_attn),
        grid=(t // tt,),
        in_specs=[
            pl.BlockSpec((bsz, tt, d), tile),
            pl.BlockSpec((1, d), c2),
            pl.BlockSpec((d, e), c2, pipeline_mode=pl.Buffered(1)),
            pl.BlockSpec((CONV_WIDTH, d_lru), c2),
            vec,
            pl.BlockSpec(wg.shape, lambda i: (0, 0, 0)),
            vec, vec, vec, vec,
        ],
        out_specs=[
            pl.BlockSpec((bsz, tt, d_lru), tile),
            pl.BlockSpec((bsz, tt, d_attn), tile),
            pl.BlockSpec((bsz, tt, d_kv2), tile),
        ],
        out_shape=[
            jax.ShapeDtypeStruct((bsz, t, d_lru), BF16),
            jax.ShapeDtypeStruct((bsz, t, d_attn), BF16),
            jax.ShapeDtypeStruct((bsz, t, d_kv2), F32),
        ],
        scratch_shapes=[
            pltpu.VMEM((d // LANES, r, LANES), F32),
            pltpu.VMEM(((CONV_WIDTH - 1) * bsz, d_lru), F32),
            pltpu.VMEM((bsz, d_lru), F32),
            pltpu.VMEM((d_lru // LANES, r, LANES), F32),
        ],
        compiler_params=pltpu.CompilerParams(
            dimension_semantics=("arbitrary",), vmem_limit_bytes=VMEM_LIMIT),
        name="in_lru",
    )(x, g, w_in_bf, conv_w, conv_b, wg, b_a, b_x, lam, out_g)


def _attn_bias(tile_rows, blk):
    qi = lax.broadcasted_iota(jnp.int32, (tile_rows, 2 * blk), 0) % blk
    kj = lax.broadcasted_iota(jnp.int32, (tile_rows, 2 * blk), 1)
    band = (kj > qi) & (kj <= qi + blk)
    return jnp.stack([jnp.where(band, 0.0, NEG_INF), jnp.where(band & (kj >= blk), 0.0, NEG_INF)]).astype(F32)


def _attn_kernel(sink_ref, bias_ref, q_ref, kv_ref, og_ref, o_ref, kvprev_ref, y_ref):
    tq, d_attn = q_ref.shape
    blk = ATTN_BLOCK
    n_blk = tq // blk
    j = pl.program_id(1)
    heads_per_tile = LANES // HEAD_DIM
    tiles_per_kv = d_attn // N_KV_HEADS // LANES

    @pl.when(j == 0)
    def _():
        kvprev_ref[...] = jnp.zeros_like(kvprev_ref)

    kv_cur = kv_ref[...]
    kv_all = jnp.concatenate([kvprev_ref[...], kv_cur], axis=0)
    kvprev_ref[...] = kv_cur[tq - blk:, :]
    k_all = kv_all[:, :LANES] * (1.0 / math.sqrt(HEAD_DIM))
    v_all = kv_all[:, LANES:]
    k_rol = pltpu.roll(k_all, HEAD_DIM, axis=1)
    v_rol = pltpu.roll(v_all, HEAD_DIM, axis=1)
    lo = lax.broadcasted_iota(jnp.int32, k_all.shape, 1) < HEAD_DIM
    ones = jnp.ones(k_all.shape, BF16)

    def half(x_same, x_rolled, kv_head, parity):
        src = x_same if kv_head == parity else x_rolled
        keep = lo if parity == 0 else jnp.logical_not(lo)
        return jnp.where(keep, src, 0.0).astype(BF16)

    k_hp = [[half(k_all, k_rol, h, p) for p in range(heads_per_tile)] for h in range(N_KV_HEADS)]
    v_hp = [[jnp.concatenate([half(v_all, v_rol, h, p), ones], axis=1) for p in range(heads_per_tile)]
            for h in range(N_KV_HEADS)]

    for i in range(n_blk):
        rows = slice(i * blk, (i + 1) * blk)
        win = slice(i * blk, (i + 2) * blk)
        bias = bias_ref[jnp.where(j == 0, 1, 0)] if i == 0 else bias_ref[0]
        for h in range(N_KV_HEADS):
            q_st = jnp.concatenate(
                [q_ref[rows, (h * tiles_per_kv + c) * LANES:(h * tiles_per_kv + c + 1) * LANES]
                 for c in range(tiles_per_kv)], axis=0)
            acc = None
            for p in range(heads_per_tile):
                s = lax.dot_general(q_st, k_hp[h][p][win], (((1,), (1,)), ((), ())),
                                    preferred_element_type=F32) + bias
                sink = jnp.concatenate(
                    [jnp.full((blk, LANES), sink_ref[(h * tiles_per_kv + c) * heads_per_tile + p], F32)
                     for c in range(tiles_per_kv)], axis=0)
                m = jnp.maximum(jnp.broadcast_to(jnp.max(s, axis=-1, keepdims=True), sink.shape), sink)
                e = jnp.concatenate([jnp.exp(s[:, :blk] - m), jnp.exp(s[:, blk:] - m)], axis=1).astype(BF16)
                ov = jnp.dot(e, v_hp[h][p][win], preferred_element_type=F32)
                denom = ov[:, LANES:] + jnp.exp(sink - m)
                o = ov[:, :LANES] / denom
                acc = o if acc is None else acc + o
            for c in range(tiles_per_kv):
                col = (h * tiles_per_kv + c) * LANES
                y_ref[rows, col:col + LANES] = acc[c * blk:(c + 1) * blk, :]
    o_ref[...] = _rms(y_ref[...], og_ref[...]).astype(BF16)


def _attn(sinks, q3, kv3, out_g):
    bsz, t, d_attn = q3.shape
    d_kv2 = kv3.shape[2]
    tq = TQ_ATTN
    tile = lambda b, j: (b, j, 0)
    bias = _attn_bias(d_attn // N_KV_HEADS // LANES * ATTN_BLOCK, ATTN_BLOCK)
    return pl.pallas_call(
        _attn_kernel,
        grid=(bsz, t // tq),
        in_specs=[
            pl.BlockSpec(memory_space=pltpu.SMEM),
            pl.BlockSpec(bias.shape, lambda b, j: (0, 0, 0), pipeline_mode=pl.Buffered(1)),
            pl.BlockSpec((None, tq, d_attn), tile),
            pl.BlockSpec((None, tq, d_kv2), tile),
            pl.BlockSpec((1, d_attn), lambda b, j: (0, 0)),
        ],
        out_specs=pl.BlockSpec((None, tq, d_attn), tile),
        out_shape=jax.ShapeDtypeStruct((bsz, t, d_attn), BF16),
        scratch_shapes=[
            pltpu.VMEM((ATTN_BLOCK, d_kv2), F32),
            pltpu.VMEM((tq, d_attn), F32),
        ],
        compiler_params=pltpu.CompilerParams(
            dimension_semantics=("arbitrary", "arbitrary"), vmem_limit_bytes=VMEM_LIMIT),
        name="attn",
    )(sinks, bias, q3, kv3, out_g)


def _out_mlp_kernel(x_ref, yl_ref, ya_ref, wo_ref, gm_ref, wu_ref, wd_ref, gf_ref, o_ref, *,
                    d_lru, ff_chunk, final_norm):
    mix = jnp.dot(yl_ref[...], wo_ref[:d_lru, :], preferred_element_type=F32)
    mix = mix + jnp.dot(ya_ref[...], wo_ref[d_lru:, :], preferred_element_type=F32)
    x1 = x_ref[...] + mix
    hm = _rms(x1, gm_ref[...]).astype(BF16)
    mlp = None
    for c in range(wu_ref.shape[1] // ff_chunk):
        cs = slice(c * ff_chunk, (c + 1) * ff_chunk)
        up = jnp.dot(hm, wu_ref[:, cs], preferred_element_type=F32)
        act = jnp.square(jnp.maximum(up, 0.0)).astype(BF16)
        down = jnp.dot(act, wd_ref[cs, :], preferred_element_type=F32)
        mlp = down if mlp is None else mlp + down
    x2 = x1 + mlp
    o_ref[...] = _rms(x2, gf_ref[...]) if final_norm else x2


def _out_mlp(x2, yl2, ya2, w_out_bf, g_mlp, w_up_bf, w_down_bf, g_final, final_norm):
    n, d = x2.shape
    d_lru, d_attn = yl2.shape[1], ya2.shape[1]
    d_ff = w_up_bf.shape[1]
    tm = TM_PROJ
    row = lambda i: (i, 0)
    const = lambda i: (0, 0)
    wspec = lambda shape: pl.BlockSpec(shape, const, pipeline_mode=pl.Buffered(1))
    return pl.pallas_call(
        functools.partial(_out_mlp_kernel, d_lru=d_lru, ff_chunk=1024, final_norm=final_norm),
        grid=(n // tm,),
        in_specs=[
            pl.BlockSpec((tm, d), row),
            pl.BlockSpec((tm, d_lru), row),
            pl.BlockSpec((tm, d_attn), row),
            wspec((d_lru + d_attn, d)),
            pl.BlockSpec((1, d), const),
            wspec((d, d_ff)),
            wspec((d_ff, d)),
            pl.BlockSpec((1, d), const),
        ],
        out_specs=pl.BlockSpec((tm, d), row),
        out_shape=jax.ShapeDtypeStruct((n, d), F32),
        compiler_params=pltpu.CompilerParams(
            dimension_semantics=("arbitrary",), vmem_limit_bytes=VMEM_LIMIT),
        name="out_mlp",
    )(x2, yl2, ya2, w_out_bf, g_mlp, w_up_bf, w_down_bf, g_final)


def _pack_gates(w_a, w_x):
    def bd(w):
        nblk, bw, _ = w.shape
        per = GATE_CHUNK // bw
        w4 = w.reshape(nblk // per, per, bw, bw)
        eye = jnp.eye(per, dtype=w.dtype)
        return jnp.einsum('cikl,ij->cikjl', w4, eye).reshape(nblk // per, GATE_CHUNK, GATE_CHUNK)
    return jnp.concatenate([bd(w_a), bd(w_x)], axis=-1).astype(BF16)


def kernel(x, norm_mix_g, w_in, conv_w, conv_b, w_gate_a, b_gate_a, w_gate_x, b_gate_x, lru_lambda,
           attn_sinks, lru_out_g, attn_out_g, w_out, norm_mlp_g, w_mlp_up, w_mlp_down, norm_final_g):
    bsz, t, d = x.shape
    depth = w_in.shape[0]
    d_lru = conv_w.shape[2]
    d_attn = attn_out_g.shape[1]
    d_kv2 = w_in.shape[2] - 2 * d_lru - d_attn
    assert d_kv2 == 2 * LANES and d_attn == N_Q_HEADS * HEAD_DIM and d_lru % GATE_CHUNK == 0
    assert bsz == SUBLANES and ATTN_BLOCK == LANES
    assert t % (TM_PROJ // bsz) == 0 and t % TM_PROJ == 0
    assert t % TQ_ATTN == 0 and TQ_ATTN % ATTN_BLOCK == 0
    n = bsz * t
    x2 = x.reshape(n, d)
    r1 = lambda v: v.reshape(1, -1)
    for l in range(depth):
        yl, q, kv = _in_lru(x2.reshape(bsz, t, d), r1(norm_mix_g[l]), w_in[l].astype(BF16), conv_w[l],
                            r1(conv_b[l]), _pack_gates(w_gate_a[l], w_gate_x[l]), r1(b_gate_a[l]),
                            r1(b_gate_x[l]), r1(lru_lambda[l]), r1(lru_out_g[l]), d_attn, d_kv2)
        ya = _attn(attn_sinks[l], q, kv, r1(attn_out_g[l]))
        x2 = _out_mlp(x2, yl.reshape(n, d_lru), ya.reshape(n, d_attn), w_out[l].astype(BF16),
                      r1(norm_mlp_g[l]), w_mlp_up[l].astype(BF16), w_mlp_down[l].astype(BF16),
                      r1(norm_final_g), final_norm=(l == depth - 1))
    return x2.reshape(bsz, t, d)
```

```python
import functools
import math

import jax
import jax.numpy as jnp
from jax import lax
from jax.experimental import pallas as pl
from jax.experimental.pallas import tpu as pltpu

F32 = jnp.float32
BF16 = jnp.bfloat16

EPS = 1e-6
NEG_INF = -1e30
LRU_C = 8.0
CONV_WIDTH = 4
HEAD_DIM = 64
N_Q_HEADS = 16
N_KV_HEADS = 2
ATTN_BLOCK = 128
LANES = 128
SUBLANES = 8
GATE_CHUNK = 256

TM_PROJ = 512
TQ_ATTN = 512
VMEM_LIMIT = 56 * 1024 * 1024


def _rms(x, g):
    ms = jnp.mean(x * x, axis=-1, keepdims=True)
    return x * lax.rsqrt(ms + EPS) * g


def _in_lru_kernel(x_ref, g_ref, w_ref, cw_ref, cb_ref, wg_ref, ba_ref, bx_ref, lam_ref, og_ref,
                   yl_ref, q_ref, kv_ref, hn_ref, tail_ref, hc_ref, y_ref, *, d_lru, d_attn):
    nb, tt, d = x_ref.shape
    r = nb * tt
    halo = (CONV_WIDTH - 1) * nb
    s1, s2, s3 = d_lru, 2 * d_lru, 2 * d_lru + d_attn

    @pl.when(pl.program_id(0) == 0)
    def _():
        tail_ref[...] = jnp.zeros_like(tail_ref)
        hc_ref[...] = jnp.zeros_like(hc_ref)

    hn_bt = []
    for b in range(nb):
        hb = _rms(x_ref[b], g_ref[...])
        hn_bt.append(hb.astype(BF16))
        for s in range(d // LANES):
            hn_ref[s, pl.ds(b, tt, stride=nb), :] = hb[:, s * LANES:(s + 1) * LANES]
    hn_bt = jnp.concatenate(hn_bt, axis=0)
    hn = jnp.concatenate([hn_ref[s] for s in range(d // LANES)], axis=1).astype(BF16)

    def project(lo_col, width=GATE_CHUNK, lhs=None):
        return jnp.dot(hn if lhs is None else lhs, w_ref[:, lo_col:lo_col + width], preferred_element_type=F32)

    n_chunks = d_lru // GATE_CHUNK
    q_cols = d_attn // n_chunks
    ssq = jnp.zeros((r, 1), F32)
    nxt = (project(0), project(s1))
    for c in range(n_chunks):
        cs = slice(c * GATE_CHUNK, (c + 1) * GATE_CHUNK)
        x, gate = nxt
        xe = jnp.concatenate([tail_ref[:, cs], x], axis=0)
        tail_ref[:, cs] = x[r - halo:, :]
        xc = cb_ref[:, cs] + cw_ref[CONV_WIDTH - 1:CONV_WIDTH, cs] * x
        for k in range(CONV_WIDTH - 1):
            xc = xc + cw_ref[k:k + 1, cs] * xe[k * nb:k * nb + r]
        z = jnp.dot(xc.astype(BF16), wg_ref[c], preferred_element_type=F32)
        if c + 1 < n_chunks:
            nxt = (project((c + 1) * GATE_CHUNK), project(s1 + (c + 1) * GATE_CHUNK))
        q_c = project(s2 + c * q_cols, q_cols, hn_bt).astype(BF16)
        q_ref[:, :, c * q_cols:(c + 1) * q_cols] = q_c.reshape(nb, tt, q_cols)
        gate_r = jax.nn.sigmoid(z[:, :GATE_CHUNK] + ba_ref[:, cs])
        gate_i = jax.nn.sigmoid(z[:, GATE_CHUNK:] + bx_ref[:, cs])
        log_a = gate_r * ((-LRU_C) * jax.nn.softplus(-lam_ref[:, cs]))
        a = jnp.exp(log_a)
        om = 1.0 - a * a
        bterm = jnp.where(om > 0.0, om * lax.rsqrt(om), 0.0) * (gate_i * xc)
        h = hc_ref[:, cs]
        hs = []
        for t in range(tt):
            h = a[t * nb:(t + 1) * nb] * h + bterm[t * nb:(t + 1) * nb]
            hs.append(h)
        hc_ref[:, cs] = h
        y = jnp.concatenate(hs, axis=0) * jax.nn.gelu(gate, approximate=True)
        for s in range(GATE_CHUNK // LANES):
            y_ref[c * (GATE_CHUNK // LANES) + s] = y[:, s * LANES:(s + 1) * LANES]
        ssq = ssq + jnp.sum(y * y, axis=-1, keepdims=True)
    kv = project(s3, kv_ref.shape[2], hn_bt)
    kv_ref[...] = kv.reshape(nb, tt, kv.shape[1])
    scale = lax.rsqrt(ssq * (1.0 / d_lru) + EPS)
    n_slab = d_lru // LANES
    for s in range(n_slab):
        y_ref[s] = y_ref[s] * scale * og_ref[:, s * LANES:(s + 1) * LANES]
    for b in range(nb):
        yl_ref[b] = jnp.concatenate(
            [y_ref[s, pl.ds(b, tt, stride=nb), :] for s in range(n_slab)], axis=1).astype(BF16)


def _in_lru(x, g, w_in_bf, conv_w, conv_b, wg, b_a, b_x, lam, out_g, d_attn, d_kv2):
    bsz, t, d = x.shape
    e = w_in_bf.shape[1]
    d_lru = conv_w.shape[1]
    tt = TM_PROJ // bsz
    r = tt * bsz
    c2 = lambda i: (0, 0)
    tile = lambda i: (0, i, 0)
    vec = pl.BlockSpec((1, d_lru), c2)
    return pl.pallas_call(
        functools.partial(_in_lru_kernel, d_lru=d_lru, d_attn=d_attn),
        grid=(t // tt,),
        in_specs=[
            pl.BlockSpec((bsz, tt, d), tile),
            pl.BlockSpec((1, d), c2),
            pl.BlockSpec((d, e), c2, pipeline_mode=pl.Buffered(1)),
            pl.BlockSpec((CONV_WIDTH, d_lru), c2),
            vec,
            pl.BlockSpec(wg.shape, lambda i: (0, 0, 0)),
            vec, vec, vec, vec,
        ],
        out_specs=[
            pl.BlockSpec((bsz, tt, d_lru), tile),
            pl.BlockSpec((bsz, tt, d_attn), tile),
            pl.BlockSpec((bsz, tt, d_kv2), tile),
        ],
        out_shape=[
            jax.ShapeDtypeStruct((bsz, t, d_lru), BF16),
            jax.ShapeDtypeStruct((bsz, t, d_attn), BF16),
            jax.ShapeDtypeStruct((bsz, t, d_kv2), F32),
        ],
        scratch_shapes=[
            pltpu.VMEM((d // LANES, r, LANES), F32),
            pltpu.VMEM(((CONV_WIDTH - 1) * bsz, d_lru), F32),
            pltpu.VMEM((bsz, d_lru), F32),
            pltpu.VMEM((d_lru // LANES, r, LANES), F32),
        ],
        compiler_params=pltpu.CompilerParams(
            dimension_semantics=("arbitrary",), vmem_limit_bytes=VMEM_LIMIT),
        name="in_lru",
    )(x, g, w_in_bf, conv_w, conv_b, wg, b_a, b_x, lam, out_g)


def _attn_bias(tile_rows, blk):
    qi = lax.broadcasted_iota(jnp.int32, (tile_rows, 2 * blk), 0) % blk
    kj = lax.broadcasted_iota(jnp.int32, (tile_rows, 2 * blk), 1)
    band = (kj > qi) & (kj <= qi + blk)
    return jnp.stack([jnp.where(band, 0.0, NEG_INF), jnp.where(band & (kj >= blk), 0.0, NEG_INF)]).astype(F32)


def _attn_mlp_kernel(sink_ref, bias_ref, q_ref, kv_ref, og_ref, x_ref, yl_ref, wo_ref, gm_ref, wu_ref, wd_ref,
                     gf_ref, o_ref, kvprev_ref, y_ref, ya_ref, *, tiles_per_seq, d_lru, final_norm):
    tq, d_attn = q_ref.shape
    blk = ATTN_BLOCK
    n_blk = tq // blk
    step = pl.program_id(0)
    j = step % tiles_per_seq
    heads_per_tile = LANES // HEAD_DIM
    tiles_per_kv = d_attn // N_KV_HEADS // LANES

    @pl.when(step == 0)
    def _():
        ya_ref[...] = jnp.zeros_like(ya_ref)

    @pl.when(j == 0)
    def _():
        kvprev_ref[...] = jnp.zeros_like(kvprev_ref)

    kv_cur = kv_ref[...]
    kv_all = jnp.concatenate([kvprev_ref[...], kv_cur], axis=0)
    kvprev_ref[...] = kv_cur[tq - blk:, :]
    k_all = kv_all[:, :LANES] * (1.0 / math.sqrt(HEAD_DIM))
    v_all = kv_all[:, LANES:]
    k_rol = pltpu.roll(k_all, HEAD_DIM, axis=1)
    v_rol = pltpu.roll(v_all, HEAD_DIM, axis=1)
    lo = lax.broadcasted_iota(jnp.int32, k_all.shape, 1) < HEAD_DIM
    ones = jnp.ones(k_all.shape, BF16)

    def half(x_same, x_rolled, kv_head, parity):
        src = x_same if kv_head == parity else x_rolled
        keep = lo if parity == 0 else jnp.logical_not(lo)
        return jnp.where(keep, src, 0.0).astype(BF16)

    k_hp = [[half(k_all, k_rol, h, p) for p in range(heads_per_tile)] for h in range(N_KV_HEADS)]
    v_hp = [[jnp.concatenate([half(v_all, v_rol, h, p), ones], axis=1) for p in range(heads_per_tile)]
            for h in range(N_KV_HEADS)]

    units = [(i, h, p) for i in range(n_blk) for h in range(N_KV_HEADS) for p in range(heads_per_tile)]

    def scores(i, h, p):
        q_st = jnp.concatenate(
            [q_ref[i * blk:(i + 1) * blk, (h * tiles_per_kv + c) * LANES:(h * tiles_per_kv + c + 1) * LANES]
             for c in range(tiles_per_kv)], axis=0)
        bias = bias_ref[jnp.where(j == 0, 1, 0)] if i == 0 else bias_ref[0]
        return lax.dot_general(q_st, k_hp[h][p][i * blk:(i + 2) * blk], (((1,), (1,)), ((), ())),
                               preferred_element_type=F32) + bias

    def softmax_numerator(i, h, p, s):
        sink = jnp.concatenate(
            [jnp.full((blk, LANES), sink_ref[(h * tiles_per_kv + c) * heads_per_tile + p], F32)
             for c in range(tiles_per_kv)], axis=0)
        m = jnp.maximum(jnp.broadcast_to(jnp.max(s, axis=-1, keepdims=True), sink.shape), sink)
        e = jnp.concatenate([jnp.exp(s[:, :blk] - m), jnp.exp(s[:, blk:] - m)], axis=1).astype(BF16)
        return e, jnp.exp(sink - m)

    def weighted_values(i, h, p, e, e_sink):
        ov = jnp.dot(e, v_hp[h][p][i * blk:(i + 2) * blk], preferred_element_type=F32)
        o = ov[:, :LANES] / (ov[:, LANES:] + e_sink)
        for c in range(tiles_per_kv):
            col = (h * tiles_per_kv + c) * LANES
            o_c = o[c * blk:(c + 1) * blk, :]
            if p == 0:
                y_ref[i * blk:(i + 1) * blk, col:col + LANES] = o_c
            else:
                y_ref[i * blk:(i + 1) * blk, col:col + LANES] += o_c

    n_ff = 4
    ff_chunk = wu_ref.shape[1] // n_ff
    n_pieces = 2 + 2 * n_ff
    state = {}

    def mlp_piece(t):
        if t == 0:
            state["mix"] = jnp.dot(yl_ref[...], wo_ref[:d_lru, :], preferred_element_type=F32)
        elif t == 1:
            mix = state.pop("mix") + jnp.dot(ya_ref[...], wo_ref[d_lru:, :], preferred_element_type=F32)
            state["x1"] = x_ref[...] + mix
            state["hm"] = _rms(state["x1"], gm_ref[...]).astype(BF16)
        elif t % 2 == 0:
            cs = slice((t - 2) // 2 * ff_chunk, ((t - 2) // 2 + 1) * ff_chunk)
            up = jnp.dot(state["hm"], wu_ref[:, cs], preferred_element_type=F32)
            state["act"] = jnp.square(jnp.maximum(up, 0.0)).astype(BF16)
        else:
            cs = slice((t - 3) // 2 * ff_chunk, ((t - 3) // 2 + 1) * ff_chunk)
            down = jnp.dot(state.pop("act"), wd_ref[cs, :], preferred_element_type=F32)
            state["mlp"] = down if t == 3 else state["mlp"] + down

    per_slot = -(-len(units) // (n_pieces - 2))
    pending = []
    for t in range(n_pieces):
        started = [(u, scores(*u)) for u in units[t * per_slot:(t + 1) * per_slot]]
        numer = [(u, softmax_numerator(*u, s)) for u, s in pending]
        mlp_piece(t)
        for u, (e, e_sink) in numer:
            weighted_values(*u, e, e_sink)
        pending = started
    assert not pending

    x2 = state["x1"] + state["mlp"]
    o_ref[...] = _rms(x2, gf_ref[...]) if final_norm else x2
    ya_ref[...] = _rms(y_ref[...], og_ref[...]).astype(BF16)


def _attn_mlp(sinks, q2, kv2, attn_g, x2, yl2, w_out_bf, g_mlp, w_up_bf, w_down_bf, g_final, tiles_per_seq,
              final_norm):
    n, d = x2.shape
    d_lru, d_attn, d_kv2 = yl2.shape[1], q2.shape[1], kv2.shape[1]
    d_ff = w_up_bf.shape[1]
    tm = TQ_ATTN
    n_tiles = n // tm
    bias = _attn_bias(d_attn // N_KV_HEADS // LANES * ATTN_BLOCK, ATTN_BLOCK)
    attn_tile = lambda s: (jnp.minimum(s, n_tiles - 1), 0)
    mlp_tile = lambda s: (jnp.maximum(s - 1, 0), 0)
    const = lambda s: (0, 0)
    wspec = lambda shape: pl.BlockSpec(shape, const, pipeline_mode=pl.Buffered(1))
    return pl.pallas_call(
        functools.partial(_attn_mlp_kernel, tiles_per_seq=tiles_per_seq, d_lru=d_lru, final_norm=final_norm),
        grid=(n_tiles + 1,),
        in_specs=[
            pl.BlockSpec(memory_space=pltpu.SMEM),
            pl.BlockSpec(bias.shape, lambda s: (0, 0, 0), pipeline_mode=pl.Buffered(1)),
            pl.BlockSpec((tm, d_attn), attn_tile),
            pl.BlockSpec((tm, d_kv2), attn_tile),
            pl.BlockSpec((1, d_attn), const),
            pl.BlockSpec((tm, d), mlp_tile),
            pl.BlockSpec((tm, d_lru), mlp_tile),
            wspec((d_lru + d_attn, d)),
            pl.BlockSpec((1, d), const),
            wspec((d, d_ff)),
            wspec((d_ff, d)),
            pl.BlockSpec((1, d), const),
        ],
        out_specs=pl.BlockSpec((tm, d), mlp_tile),
        out_shape=jax.ShapeDtypeStruct((n, d), F32),
        scratch_shapes=[
            pltpu.VMEM((ATTN_BLOCK, d_kv2), F32),
            pltpu.VMEM((tm, d_attn), F32),
            pltpu.VMEM((tm, d_attn), BF16),
        ],
        compiler_params=pltpu.CompilerParams(
            dimension_semantics=("arbitrary",), vmem_limit_bytes=VMEM_LIMIT),
        name="attn_mlp",
    )(sinks, bias, q2, kv2, attn_g, x2, yl2, w_out_bf, g_mlp, w_up_bf, w_down_bf, g_final)


def _pack_gates(w_a, w_x):
    def bd(w):
        nblk, bw, _ = w.shape
        per = GATE_CHUNK // bw
        w4 = w.reshape(nblk // per, per, bw, bw)
        eye = jnp.eye(per, dtype=w.dtype)
        return jnp.einsum('cikl,ij->cikjl', w4, eye).reshape(nblk // per, GATE_CHUNK, GATE_CHUNK)
    return jnp.concatenate([bd(w_a), bd(w_x)], axis=-1).astype(BF16)


def kernel(x, norm_mix_g, w_in, conv_w, conv_b, w_gate_a, b_gate_a, w_gate_x, b_gate_x, lru_lambda,
           attn_sinks, lru_out_g, attn_out_g, w_out, norm_mlp_g, w_mlp_up, w_mlp_down, norm_final_g):
    bsz, t, d = x.shape
    depth = w_in.shape[0]
    d_lru = conv_w.shape[2]
    d_attn = attn_out_g.shape[1]
    d_kv2 = w_in.shape[2] - 2 * d_lru - d_attn
    assert d_kv2 == 2 * LANES and d_attn == N_Q_HEADS * HEAD_DIM and d_lru % GATE_CHUNK == 0
    assert bsz == SUBLANES and ATTN_BLOCK == LANES
    assert t % (TM_PROJ // bsz) == 0 and t % TM_PROJ == 0
    assert t % TQ_ATTN == 0 and TQ_ATTN % ATTN_BLOCK == 0
    n = bsz * t
    x2 = x.reshape(n, d)
    r1 = lambda v: v.reshape(1, -1)
    for l in range(depth):
        yl, q, kv = _in_lru(x2.reshape(bsz, t, d), r1(norm_mix_g[l]), w_in[l].astype(BF16), conv_w[l],
                            r1(conv_b[l]), _pack_gates(w_gate_a[l], w_gate_x[l]), r1(b_gate_a[l]),
                            r1(b_gate_x[l]), r1(lru_lambda[l]), r1(lru_out_g[l]), d_attn, d_kv2)
        x2 = _attn_mlp(attn_sinks[l], q.reshape(n, d_attn), kv.reshape(n, d_kv2), r1(attn_out_g[l]), x2,
                       yl.reshape(n, d_lru), w_out[l].astype(BF16), r1(norm_mlp_g[l]), w_mlp_up[l].astype(BF16),
                       w_mlp_down[l].astype(BF16), r1(norm_final_g), tiles_per_seq=t // TQ_ATTN,
                       final_norm=(l == depth - 1))
    return x2.reshape(bsz, t, d)
```

```python
import functools
import math

import jax
import jax.numpy as jnp
from jax import lax
from jax.experimental import pallas as pl
from jax.experimental.pallas import tpu as pltpu

F32 = jnp.float32
BF16 = jnp.bfloat16

EPS = 1e-6
NEG_INF = -1e30
LRU_C = 8.0
GELU_K1 = 2.0 * math.sqrt(2.0 / math.pi)
GELU_K3 = GELU_K1 * 0.044715
CONV_WIDTH = 4
HEAD_DIM = 64
N_Q_HEADS = 16
N_KV_HEADS = 2
ATTN_BLOCK = 128
LANES = 128
SUBLANES = 8
BF16_ROWS = 16
GATE_CHUNK = 256

TM_PROJ = 512
TQ_ATTN = 512
VMEM_LIMIT = 56 * 1024 * 1024


def _rms(x, g):
    ms = jnp.mean(x * x, axis=-1, keepdims=True)
    return x * lax.rsqrt(ms + EPS) * g


def _in_lru_kernel(x_ref, g_ref, w32_ref, cw_ref, cb_ref, wg_ref, ba_ref, bx_ref, lam_ref, og_ref,
                   wo32_ref, wu32_ref, wd32_ref,
                   yl_ref, q_ref, kv_ref, wo_ref, wu_ref, wd_ref,
                   w_ref, hn_ref, tail_ref, hc_ref, y_ref, *, d_lru, d_attn):
    nb, tt, d = x_ref.shape
    r = nb * tt
    halo = (CONV_WIDTH - 1) * nb
    s1, s2, s3 = d_lru, 2 * d_lru, 2 * d_lru + d_attn

    @pl.when(pl.program_id(0) == 0)
    def _():
        tail_ref[...] = jnp.zeros_like(tail_ref)
        hc_ref[...] = jnp.zeros_like(hc_ref)
        w_ref[...] = w32_ref[...].astype(BF16)

    wo_ref[...] = wo32_ref[...].astype(BF16)
    wu_ref[...] = wu32_ref[...].astype(BF16)
    wd_ref[...] = wd32_ref[...].astype(BF16)

    hn_bt = []
    for b in range(nb):
        hb = _rms(x_ref[b], g_ref[...])
        hn_bt.append(hb.astype(BF16))
        for s in range(d // LANES):
            hn_ref[s, pl.ds(b, tt, stride=nb), :] = hb[:, s * LANES:(s + 1) * LANES]
    hn_bt = jnp.concatenate(hn_bt, axis=0)
    hn = jnp.concatenate([hn_ref[s] for s in range(d // LANES)], axis=1).astype(BF16)

    def project(lo_col, width=GATE_CHUNK, lhs=None):
        return jnp.dot(hn if lhs is None else lhs, w_ref[:, lo_col:lo_col + width], preferred_element_type=F32)

    n_chunks = d_lru // GATE_CHUNK
    q_cols = d_attn // n_chunks
    ssq = jnp.zeros((r, 1), F32)
    nxt = (project(0), project(s1))
    for c in range(n_chunks):
        cs = slice(c * GATE_CHUNK, (c + 1) * GATE_CHUNK)
        x, gate = nxt
        xe = jnp.concatenate([tail_ref[:, cs], x], axis=0)
        tail_ref[:, cs] = x[r - halo:, :]
        xc = cb_ref[:, cs] + cw_ref[CONV_WIDTH - 1:CONV_WIDTH, cs] * x
        for k in range(CONV_WIDTH - 1):
            xc = xc + cw_ref[k:k + 1, cs] * xe[k * nb:k * nb + r]
        z = jnp.dot(xc.astype(BF16), wg_ref[c], preferred_element_type=F32)
        if c + 1 < n_chunks:
            nxt = (project((c + 1) * GATE_CHUNK), project(s1 + (c + 1) * GATE_CHUNK))
        q_c = project(s2 + c * q_cols, q_cols, hn_bt).astype(BF16)
        q_ref[:, :, c * q_cols:(c + 1) * q_cols] = q_c.reshape(nb, tt, q_cols)
        gate_r = jax.nn.sigmoid(z[:, :GATE_CHUNK] + ba_ref[:, cs])
        gate_i = jax.nn.sigmoid(z[:, GATE_CHUNK:] + bx_ref[:, cs])
        a = jnp.exp2(gate_r * ((-LRU_C * math.log2(math.e)) * jax.nn.softplus(-lam_ref[:, cs])))
        om = 1.0 - a * a
        bterm = jnp.where(om > 0.0, om * lax.rsqrt(om), 0.0) * (gate_i * xc)
        h = hc_ref[:, cs]
        hs = []
        for t in range(tt):
            h = a[t * nb:(t + 1) * nb] * h + bterm[t * nb:(t + 1) * nb]
            hs.append(h)
        hc_ref[:, cs] = h
        gelu = gate * jax.nn.sigmoid(gate * (GELU_K1 + GELU_K3 * (gate * gate)))
        y = jnp.concatenate(hs, axis=0) * gelu
        for s in range(GATE_CHUNK // LANES):
            y_ref[c * (GATE_CHUNK // LANES) + s] = y[:, s * LANES:(s + 1) * LANES]
        ssq = ssq + jnp.sum(y * y, axis=-1, keepdims=True)
    kv = project(s3, kv_ref.shape[2], hn_bt)
    kv_ref[...] = kv.reshape(nb, tt, kv.shape[1])
    scale = lax.rsqrt(ssq * (1.0 / d_lru) + EPS)
    n_slab = d_lru // LANES
    for s in range(n_slab):
        y_ref[s] = y_ref[s] * scale * og_ref[:, s * LANES:(s + 1) * LANES]
    for b in range(nb):
        yl_ref[b] = jnp.concatenate(
            [y_ref[s, pl.ds(b, tt, stride=nb), :] for s in range(n_slab)], axis=1).astype(BF16)


def _in_lru(x, g, w_in, conv_w, conv_b, wg, b_a, b_x, lam, out_g, w_out, w_up, w_down, d_attn, d_kv2):
    bsz, t, d = x.shape
    e = w_in.shape[1]
    d_lru = conv_w.shape[1]
    tt = TM_PROJ // bsz
    r = tt * bsz
    n_steps = t // tt
    slab = lambda w: pl.BlockSpec((w.shape[0] // n_steps, w.shape[1]), lambda i: (i, 0))
    for w in (w_out, w_up, w_down):
        assert w.shape[0] % (n_steps * BF16_ROWS) == 0
    c2 = lambda i: (0, 0)
    tile = lambda i: (0, i, 0)
    vec = pl.BlockSpec((1, d_lru), c2)
    return pl.pallas_call(
        functools.partial(_in_lru_kernel, d_lru=d_lru, d_attn=d_attn),
        grid=(t // tt,),
        in_specs=[
            pl.BlockSpec((bsz, tt, d), tile),
            pl.BlockSpec((1, d), c2),
            pl.BlockSpec((d, e), c2, pipeline_mode=pl.Buffered(1)),
            pl.BlockSpec((CONV_WIDTH, d_lru), c2),
            vec,
            pl.BlockSpec(wg.shape, lambda i: (0, 0, 0)),
            vec, vec, vec, vec,
            slab(w_out), slab(w_up), slab(w_down),
        ],
        out_specs=[
            pl.BlockSpec((bsz, tt, d_lru), tile),
            pl.BlockSpec((bsz, tt, d_attn), tile),
            pl.BlockSpec((bsz, tt, d_kv2), tile),
            slab(w_out), slab(w_up), slab(w_down),
        ],
        out_shape=[
            jax.ShapeDtypeStruct((bsz, t, d_lru), BF16),
            jax.ShapeDtypeStruct((bsz, t, d_attn), BF16),
            jax.ShapeDtypeStruct((bsz, t, d_kv2), F32),
            jax.ShapeDtypeStruct(w_out.shape, BF16),
            jax.ShapeDtypeStruct(w_up.shape, BF16),
            jax.ShapeDtypeStruct(w_down.shape, BF16),
        ],
        scratch_shapes=[
            pltpu.VMEM((d, e), BF16),
            pltpu.VMEM((d // LANES, r, LANES), F32),
            pltpu.VMEM(((CONV_WIDTH - 1) * bsz, d_lru), F32),
            pltpu.VMEM((bsz, d_lru), F32),
            pltpu.VMEM((d_lru // LANES, r, LANES), F32),
        ],
        compiler_params=pltpu.CompilerParams(
            dimension_semantics=("arbitrary",), vmem_limit_bytes=VMEM_LIMIT),
        name="in_lru",
    )(x, g, w_in, conv_w, conv_b, wg, b_a, b_x, lam, out_g, w_out, w_up, w_down)


def _attn_bias(tile_rows, blk):
    qi = lax.broadcasted_iota(jnp.int32, (tile_rows, 2 * blk), 0) % blk
    kj = lax.broadcasted_iota(jnp.int32, (tile_rows, 2 * blk), 1)
    band = (kj > qi) & (kj <= qi + blk)
    return jnp.stack([jnp.where(band, 0.0, NEG_INF), jnp.where(band & (kj >= blk), 0.0, NEG_INF)]).astype(F32)


def _attn_mlp_kernel(sink_ref, bias_ref, q_ref, kv_ref, og_ref, x_ref, yl_ref, wo_ref, gm_ref, wu_ref, wd_ref,
                     gf_ref, o_ref, kvprev_ref, y_ref, ya_ref, *, tiles_per_seq, d_lru, final_norm):
    tq, d_attn = q_ref.shape
    blk = ATTN_BLOCK
    n_blk = tq // blk
    step = pl.program_id(0)
    j = step % tiles_per_seq
    heads_per_tile = LANES // HEAD_DIM
    tiles_per_kv = d_attn // N_KV_HEADS // LANES

    @pl.when(step == 0)
    def _():
        ya_ref[...] = jnp.zeros_like(ya_ref)

    @pl.when(j == 0)
    def _():
        kvprev_ref[...] = jnp.zeros_like(kvprev_ref)

    kv_cur = kv_ref[...]
    kv_all = jnp.concatenate([kvprev_ref[...], kv_cur], axis=0)
    kvprev_ref[...] = kv_cur[tq - blk:, :]
    k_all = kv_all[:, :LANES] * (1.0 / math.sqrt(HEAD_DIM))
    v_all = kv_all[:, LANES:]
    k_rol = pltpu.roll(k_all, HEAD_DIM, axis=1)
    v_rol = pltpu.roll(v_all, HEAD_DIM, axis=1)
    lo = lax.broadcasted_iota(jnp.int32, k_all.shape, 1) < HEAD_DIM
    ones = jnp.ones(k_all.shape, BF16)

    def half(x_same, x_rolled, kv_head, parity):
        src = x_same if kv_head == parity else x_rolled
        keep = lo if parity == 0 else jnp.logical_not(lo)
        return jnp.where(keep, src, 0.0).astype(BF16)

    k_hp = [[half(k_all, k_rol, h, p) for p in range(heads_per_tile)] for h in range(N_KV_HEADS)]
    v_hp = [[jnp.concatenate([half(v_all, v_rol, h, p), ones], axis=1) for p in range(heads_per_tile)]
            for h in range(N_KV_HEADS)]

    units = [(i, h, p) for i in range(n_blk) for h in range(N_KV_HEADS) for p in range(heads_per_tile)]

    def scores(i, h, p):
        q_st = jnp.concatenate(
            [q_ref[i * blk:(i + 1) * blk, (h * tiles_per_kv + c) * LANES:(h * tiles_per_kv + c + 1) * LANES]
             for c in range(tiles_per_kv)], axis=0)
        bias = bias_ref[jnp.where(j == 0, 1, 0)] if i == 0 else bias_ref[0]
        return lax.dot_general(q_st, k_hp[h][p][i * blk:(i + 2) * blk], (((1,), (1,)), ((), ())),
                               preferred_element_type=F32) + bias

    def softmax_numerator(i, h, p, s):
        sink = jnp.concatenate(
            [jnp.full((blk, LANES), sink_ref[(h * tiles_per_kv + c) * heads_per_tile + p], F32)
             for c in range(tiles_per_kv)], axis=0)
        m = jnp.maximum(jnp.broadcast_to(jnp.max(s, axis=-1, keepdims=True), sink.shape), sink)
        e = jnp.concatenate([jnp.exp(s[:, :blk] - m), jnp.exp(s[:, blk:] - m)], axis=1).astype(BF16)
        return e, jnp.exp(sink - m)

    def weighted_values(i, h, p, e, e_sink):
        ov = jnp.dot(e, v_hp[h][p][i * blk:(i + 2) * blk], preferred_element_type=F32)
        o = ov[:, :LANES] / (ov[:, LANES:] + e_sink)
        for c in range(tiles_per_kv):
            col = (h * tiles_per_kv + c) * LANES
            o_c = o[c * blk:(c + 1) * blk, :]
            if p == 0:
                y_ref[i * blk:(i + 1) * blk, col:col + LANES] = o_c
            else:
                y_ref[i * blk:(i + 1) * blk, col:col + LANES] += o_c

    n_ff = 4
    ff_chunk = wu_ref.shape[1] // n_ff
    n_pieces = 2 + 2 * n_ff
    state = {}

    def mlp_piece(t):
        if t == 0:
            state["mix"] = jnp.dot(yl_ref[...], wo_ref[:d_lru, :], preferred_element_type=F32)
        elif t == 1:
            mix = state.pop("mix") + jnp.dot(ya_ref[...], wo_ref[d_lru:, :], preferred_element_type=F32)
            state["x1"] = x_ref[...] + mix
            state["hm"] = _rms(state["x1"], gm_ref[...]).astype(BF16)
        elif t % 2 == 0:
            cs = slice((t - 2) // 2 * ff_chunk, ((t - 2) // 2 + 1) * ff_chunk)
            up = jnp.dot(state["hm"], wu_ref[:, cs], preferred_element_type=F32)
            state["act"] = jnp.square(jnp.maximum(up, 0.0)).astype(BF16)
        else:
            cs = slice((t - 3) // 2 * ff_chunk, ((t - 3) // 2 + 1) * ff_chunk)
            down = jnp.dot(state.pop("act"), wd_ref[cs, :], preferred_element_type=F32)
            state["mlp"] = down if t == 3 else state["mlp"] + down

    per_slot = -(-len(units) // (n_pieces - 2))
    pending = []
    for t in range(n_pieces):
        started = [(u, scores(*u)) for u in units[t * per_slot:(t + 1) * per_slot]]
        numer = [(u, softmax_numerator(*u, s)) for u, s in pending]
        mlp_piece(t)
        for u, (e, e_sink) in numer:
            weighted_values(*u, e, e_sink)
        pending = started
    assert not pending

    x2 = state["x1"] + state["mlp"]
    o_ref[...] = _rms(x2, gf_ref[...]) if final_norm else x2
    ya_ref[...] = _rms(y_ref[...], og_ref[...]).astype(BF16)


def _attn_mlp(sinks, q2, kv2, attn_g, x2, yl2, w_out_bf, g_mlp, w_up_bf, w_down_bf, g_final, tiles_per_seq,
              final_norm):
    n, d = x2.shape
    d_lru, d_attn, d_kv2 = yl2.shape[1], q2.shape[1], kv2.shape[1]
    d_ff = w_up_bf.shape[1]
    tm = TQ_ATTN
    n_tiles = n // tm
    bias = _attn_bias(d_attn // N_KV_HEADS // LANES * ATTN_BLOCK, ATTN_BLOCK)
    attn_tile = lambda s: (jnp.minimum(s, n_tiles - 1), 0)
    mlp_tile = lambda s: (jnp.maximum(s - 1, 0), 0)
    const = lambda s: (0, 0)
    wspec = lambda shape: pl.BlockSpec(shape, const, pipeline_mode=pl.Buffered(1))
    return pl.pallas_call(
        functools.partial(_attn_mlp_kernel, tiles_per_seq=tiles_per_seq, d_lru=d_lru, final_norm=final_norm),
        grid=(n_tiles + 1,),
        in_specs=[
            pl.BlockSpec(memory_space=pltpu.SMEM),
            pl.BlockSpec(bias.shape, lambda s: (0, 0, 0), pipeline_mode=pl.Buffered(1)),
            pl.BlockSpec((tm, d_attn), attn_tile),
            pl.BlockSpec((tm, d_kv2), attn_tile),
            pl.BlockSpec((1, d_attn), const),
            pl.BlockSpec((tm, d), mlp_tile),
            pl.BlockSpec((tm, d_lru), mlp_tile),
            wspec((d_lru + d_attn, d)),
            pl.BlockSpec((1, d), const),
            wspec((d, d_ff)),
            wspec((d_ff, d)),
            pl.BlockSpec((1, d), const),
        ],
        out_specs=pl.BlockSpec((tm, d), mlp_tile),
        out_shape=jax.ShapeDtypeStruct((n, d), F32),
        scratch_shapes=[
            pltpu.VMEM((ATTN_BLOCK, d_kv2), F32),
            pltpu.VMEM((tm, d_attn), F32),
            pltpu.VMEM((tm, d_attn), BF16),
        ],
        compiler_params=pltpu.CompilerParams(
            dimension_semantics=("arbitrary",), vmem_limit_bytes=VMEM_LIMIT),
        name="attn_mlp",
    )(sinks, bias, q2, kv2, attn_g, x2, yl2, w_out_bf, g_mlp, w_up_bf, w_down_bf, g_final)


def _pack_gates(w_a, w_x):
    def bd(w):
        nblk, bw, _ = w.shape
        per = GATE_CHUNK // bw
        w4 = w.reshape(nblk // per, per, bw, bw)
        eye = jnp.eye(per, dtype=w.dtype)
        return jnp.einsum('cikl,ij->cikjl', w4, eye).reshape(nblk // per, GATE_CHUNK, GATE_CHUNK)
    return jnp.concatenate([bd(w_a), bd(w_x)], axis=-1).astype(BF16)


def kernel(x, norm_mix_g, w_in, conv_w, conv_b, w_gate_a, b_gate_a, w_gate_x, b_gate_x, lru_lambda,
           attn_sinks, lru_out_g, attn_out_g, w_out, norm_mlp_g, w_mlp_up, w_mlp_down, norm_final_g):
    bsz, t, d = x.shape
    depth = w_in.shape[0]
    d_lru = conv_w.shape[2]
    d_attn = attn_out_g.shape[1]
    d_kv2 = w_in.shape[2] - 2 * d_lru - d_attn
    assert d_kv2 == 2 * LANES and d_attn == N_Q_HEADS * HEAD_DIM and d_lru % GATE_CHUNK == 0
    assert bsz == SUBLANES and ATTN_BLOCK == LANES
    assert t % (TM_PROJ // bsz) == 0 and t % TM_PROJ == 0
    assert t % TQ_ATTN == 0 and TQ_ATTN % ATTN_BLOCK == 0
    n = bsz * t
    x2 = x.reshape(n, d)
    r1 = lambda v: v.reshape(1, -1)
    for l in range(depth):
        yl, q, kv, w_out_bf, w_up_bf, w_down_bf = _in_lru(
            x2.reshape(bsz, t, d), r1(norm_mix_g[l]), w_in[l], conv_w[l], r1(conv_b[l]),
            _pack_gates(w_gate_a[l], w_gate_x[l]), r1(b_gate_a[l]), r1(b_gate_x[l]), r1(lru_lambda[l]),
            r1(lru_out_g[l]), w_out[l], w_mlp_up[l], w_mlp_down[l], d_attn, d_kv2)
        x2 = _attn_mlp(attn_sinks[l], q.reshape(n, d_attn), kv.reshape(n, d_kv2), r1(attn_out_g[l]), x2,
                       yl.reshape(n, d_lru), w_out_bf, r1(norm_mlp_g[l]), w_up_bf, w_down_bf, r1(norm_final_g),
                       tiles_per_seq=t // TQ_ATTN, final_norm=(l == depth - 1))
    return x2.reshape(bsz, t, d)
```

```python
import functools
import math

import jax
import jax.numpy as jnp
from jax import lax
from jax.experimental import pallas as pl
from jax.experimental.pallas import tpu as pltpu

F32 = jnp.float32
BF16 = jnp.bfloat16

EPS = 1e-6
NEG_INF = -1e30
LRU_C = 8.0
GELU_K1 = 2.0 * math.sqrt(2.0 / math.pi)
GELU_K3 = GELU_K1 * 0.044715
CONV_WIDTH = 4
HEAD_DIM = 64
N_Q_HEADS = 16
N_KV_HEADS = 2
ATTN_BLOCK = 128
LANES = 128
SUBLANES = 8
BF16_ROWS = 16
GATE_CHUNK = 256

TM_PROJ = 512
TQ_ATTN = 512
VMEM_LIMIT = 56 * 1024 * 1024


def _rms(x, g):
    ms = jnp.mean(x * x, axis=-1, keepdims=True)
    return x * lax.rsqrt(ms + EPS) * g


def _in_lru_kernel(x_ref, g_ref, w32_ref, cw_ref, cb_ref, wg_ref, ba_ref, bx_ref, lam_ref,
                   wo32_ref, wu32_ref, wd32_ref,
                   h_ref, gl_ref, q_ref, kv_ref, wo_ref, wu_ref, wd_ref,
                   w_ref, hn_ref, tail_ref, hc_ref, y_ref, *, d_lru, d_attn):
    nb, tt, d = x_ref.shape
    r = nb * tt
    halo = (CONV_WIDTH - 1) * nb
    s1, s2, s3 = d_lru, 2 * d_lru, 2 * d_lru + d_attn

    @pl.when(pl.program_id(0) == 0)
    def _():
        tail_ref[...] = jnp.zeros_like(tail_ref)
        hc_ref[...] = jnp.zeros_like(hc_ref)
        w_ref[...] = w32_ref[...].astype(BF16)

    wo_ref[...] = wo32_ref[...].astype(BF16)
    wu_ref[...] = wu32_ref[...].astype(BF16)
    wd_ref[...] = wd32_ref[...].astype(BF16)

    hn_bt = []
    for b in range(nb):
        hb = _rms(x_ref[b], g_ref[...])
        hn_bt.append(hb.astype(BF16))
        for s in range(d // LANES):
            hn_ref[s, pl.ds(b, tt, stride=nb), :] = hb[:, s * LANES:(s + 1) * LANES]
    hn_bt = jnp.concatenate(hn_bt, axis=0)
    hn = jnp.concatenate([hn_ref[s] for s in range(d // LANES)], axis=1).astype(BF16)

    def project(lo_col, width, lhs):
        return jnp.dot(lhs, w_ref[:, lo_col:lo_col + width], preferred_element_type=F32)

    n_chunks = d_lru // GATE_CHUNK
    q_cols = d_attn // n_chunks
    nxt = project(0, GATE_CHUNK, hn)
    for c in range(n_chunks):
        cs = slice(c * GATE_CHUNK, (c + 1) * GATE_CHUNK)
        x = nxt
        xe = jnp.concatenate([tail_ref[:, cs], x], axis=0)
        tail_ref[:, cs] = x[r - halo:, :]
        xc = cb_ref[:, cs] + cw_ref[CONV_WIDTH - 1:CONV_WIDTH, cs] * x
        for k in range(CONV_WIDTH - 1):
            xc = xc + cw_ref[k:k + 1, cs] * xe[k * nb:k * nb + r]
        z = jnp.dot(xc.astype(BF16), wg_ref[c], preferred_element_type=F32)
        if c + 1 < n_chunks:
            nxt = project((c + 1) * GATE_CHUNK, GATE_CHUNK, hn)
        gl_c = project(s1 + c * GATE_CHUNK, GATE_CHUNK, hn_bt).astype(BF16)
        gl_ref[:, :, cs] = gl_c.reshape(nb, tt, GATE_CHUNK)
        q_c = project(s2 + c * q_cols, q_cols, hn_bt).astype(BF16)
        q_ref[:, :, c * q_cols:(c + 1) * q_cols] = q_c.reshape(nb, tt, q_cols)
        gate_r = jax.nn.sigmoid(z[:, :GATE_CHUNK] + ba_ref[:, cs])
        gate_i = jax.nn.sigmoid(z[:, GATE_CHUNK:] + bx_ref[:, cs])
        a = jnp.exp2(gate_r * ((-LRU_C * math.log2(math.e)) * jax.nn.softplus(-lam_ref[:, cs])))
        om = 1.0 - a * a
        bterm = jnp.where(om > 0.0, om * lax.rsqrt(om), 0.0) * (gate_i * xc)
        h = hc_ref[:, cs]
        for t in range(tt):
            h = a[t * nb:(t + 1) * nb] * h + bterm[t * nb:(t + 1) * nb]
            for s in range(GATE_CHUNK // LANES):
                y_ref[c * (GATE_CHUNK // LANES) + s, t * nb:(t + 1) * nb, :] = h[:, s * LANES:(s + 1) * LANES]
        hc_ref[:, cs] = h
    kv = project(s3, kv_ref.shape[2], hn_bt)
    kv_ref[...] = kv.reshape(nb, tt, kv.shape[1])
    for b in range(nb):
        h_ref[b] = jnp.concatenate(
            [y_ref[s, pl.ds(b, tt, stride=nb), :] for s in range(d_lru // LANES)], axis=1).astype(BF16)


def _in_lru(x, g, w_in, conv_w, conv_b, wg, b_a, b_x, lam, w_out, w_up, w_down, d_attn, d_kv2):
    bsz, t, d = x.shape
    e = w_in.shape[1]
    d_lru = conv_w.shape[1]
    tt = TM_PROJ // bsz
    r = tt * bsz
    n_steps = t // tt
    slab = lambda w: pl.BlockSpec((w.shape[0] // n_steps, w.shape[1]), lambda i: (i, 0))
    for w in (w_out, w_up, w_down):
        assert w.shape[0] % (n_steps * BF16_ROWS) == 0
    c2 = lambda i: (0, 0)
    tile = lambda i: (0, i, 0)
    vec = pl.BlockSpec((1, d_lru), c2)
    return pl.pallas_call(
        functools.partial(_in_lru_kernel, d_lru=d_lru, d_attn=d_attn),
        grid=(t // tt,),
        in_specs=[
            pl.BlockSpec((bsz, tt, d), tile),
            pl.BlockSpec((1, d), c2),
            pl.BlockSpec((d, e), c2, pipeline_mode=pl.Buffered(1)),
            pl.BlockSpec((CONV_WIDTH, d_lru), c2),
            vec,
            pl.BlockSpec(wg.shape, lambda i: (0, 0, 0)),
            vec, vec, vec,
            slab(w_out), slab(w_up), slab(w_down),
        ],
        out_specs=[
            pl.BlockSpec((bsz, tt, d_lru), tile),
            pl.BlockSpec((bsz, tt, d_lru), tile),
            pl.BlockSpec((bsz, tt, d_attn), tile),
            pl.BlockSpec((bsz, tt, d_kv2), tile),
            slab(w_out), slab(w_up), slab(w_down),
        ],
        out_shape=[
            jax.ShapeDtypeStruct((bsz, t, d_lru), BF16),
            jax.ShapeDtypeStruct((bsz, t, d_lru), BF16),
            jax.ShapeDtypeStruct((bsz, t, d_attn), BF16),
            jax.ShapeDtypeStruct((bsz, t, d_kv2), F32),
            jax.ShapeDtypeStruct(w_out.shape, BF16),
            jax.ShapeDtypeStruct(w_up.shape, BF16),
            jax.ShapeDtypeStruct(w_down.shape, BF16),
        ],
        scratch_shapes=[
            pltpu.VMEM((d, e), BF16),
            pltpu.VMEM((d // LANES, r, LANES), F32),
            pltpu.VMEM(((CONV_WIDTH - 1) * bsz, d_lru), F32),
            pltpu.VMEM((bsz, d_lru), F32),
            pltpu.VMEM((d_lru // LANES, r, LANES), F32),
        ],
        compiler_params=pltpu.CompilerParams(
            dimension_semantics=("arbitrary",), vmem_limit_bytes=VMEM_LIMIT),
        name="in_lru",
    )(x, g, w_in, conv_w, conv_b, wg, b_a, b_x, lam, w_out, w_up, w_down)


def _attn_bias(tile_rows, blk):
    qi = lax.broadcasted_iota(jnp.int32, (tile_rows, 2 * blk), 0) % blk
    kj = lax.broadcasted_iota(jnp.int32, (tile_rows, 2 * blk), 1)
    band = (kj > qi) & (kj <= qi + blk)
    return jnp.stack([jnp.where(band, 0.0, NEG_INF), jnp.where(band & (kj >= blk), 0.0, NEG_INF)]).astype(F32)


def _attn_mlp_kernel(sink_ref, bias_ref, q_ref, kv_ref, og_ref, h_ref, gl_ref, lg_ref, x_ref, wo_ref, gm_ref,
                     wu_ref, wd_ref, gf_ref, o_ref, kvprev_ref, y_ref, ya_ref, yl_ref, *, tiles_per_seq, d_lru,
                     final_norm):
    tq, d_attn = q_ref.shape
    blk = ATTN_BLOCK
    n_blk = tq // blk
    step = pl.program_id(0)
    j = step % tiles_per_seq
    heads_per_tile = LANES // HEAD_DIM
    tiles_per_kv = d_attn // N_KV_HEADS // LANES

    @pl.when(step == 0)
    def _():
        ya_ref[...] = jnp.zeros_like(ya_ref)
        yl_ref[...] = jnp.zeros_like(yl_ref)

    @pl.when(j == 0)
    def _():
        kvprev_ref[...] = jnp.zeros_like(kvprev_ref)

    kv_cur = kv_ref[...]
    kv_all = jnp.concatenate([kvprev_ref[...], kv_cur], axis=0)
    kvprev_ref[...] = kv_cur[tq - blk:, :]
    k_all = kv_all[:, :LANES] * (1.0 / math.sqrt(HEAD_DIM))
    v_all = kv_all[:, LANES:]
    k_rol = pltpu.roll(k_all, HEAD_DIM, axis=1)
    v_rol = pltpu.roll(v_all, HEAD_DIM, axis=1)
    lo = lax.broadcasted_iota(jnp.int32, k_all.shape, 1) < HEAD_DIM
    ones = jnp.ones(k_all.shape, BF16)

    def half(x_same, x_rolled, kv_head, parity):
        src = x_same if kv_head == parity else x_rolled
        keep = lo if parity == 0 else jnp.logical_not(lo)
        return jnp.where(keep, src, 0.0).astype(BF16)

    k_hp = [[half(k_all, k_rol, h, p) for p in range(heads_per_tile)] for h in range(N_KV_HEADS)]
    v_hp = [[jnp.concatenate([half(v_all, v_rol, h, p), ones], axis=1) for p in range(heads_per_tile)]
            for h in range(N_KV_HEADS)]

    units = [(i, h, p) for i in range(n_blk) for h in range(N_KV_HEADS) for p in range(heads_per_tile)]

    def scores(i, h, p):
        q_st = jnp.concatenate(
            [q_ref[i * blk:(i + 1) * blk, (h * tiles_per_kv + c) * LANES:(h * tiles_per_kv + c + 1) * LANES]
             for c in range(tiles_per_kv)], axis=0)
        bias = bias_ref[jnp.where(j == 0, 1, 0)] if i == 0 else bias_ref[0]
        return lax.dot_general(q_st, k_hp[h][p][i * blk:(i + 2) * blk], (((1,), (1,)), ((), ())),
                               preferred_element_type=F32) + bias

    def softmax_numerator(i, h, p, s):
        sink = jnp.concatenate(
            [jnp.full((blk, LANES), sink_ref[(h * tiles_per_kv + c) * heads_per_tile + p], F32)
             for c in range(tiles_per_kv)], axis=0)
        m = jnp.maximum(jnp.broadcast_to(jnp.max(s, axis=-1, keepdims=True), sink.shape), sink)
        e = jnp.concatenate([jnp.exp(s[:, :blk] - m), jnp.exp(s[:, blk:] - m)], axis=1).astype(BF16)
        return e, jnp.exp(sink - m)

    def weighted_values(i, h, p, e, e_sink):
        ov = jnp.dot(e, v_hp[h][p][i * blk:(i + 2) * blk], preferred_element_type=F32)
        o = ov[:, :LANES] / (ov[:, LANES:] + e_sink)
        for c in range(tiles_per_kv):
            col = (h * tiles_per_kv + c) * LANES
            o_c = o[c * blk:(c + 1) * blk, :]
            if p == 0:
                y_ref[i * blk:(i + 1) * blk, col:col + LANES] = o_c
            else:
                y_ref[i * blk:(i + 1) * blk, col:col + LANES] += o_c

    def lru_finish(rb):
        rows = slice(rb * blk, (rb + 1) * blk)
        g = gl_ref[rows, :].astype(F32)
        y = h_ref[rows, :].astype(F32) * (g * jax.nn.sigmoid(g * (GELU_K1 + GELU_K3 * (g * g))))
        yl_ref[rows, :] = _rms(y, lg_ref[...]).astype(BF16)

    n_ff = 4
    ff_chunk = wu_ref.shape[1] // n_ff
    n_pieces = 2 + 2 * n_ff
    state = {}

    def mlp_piece(t):
        if t == 0:
            state["mix"] = jnp.dot(yl_ref[...], wo_ref[:d_lru, :], preferred_element_type=F32)
        elif t == 1:
            mix = state.pop("mix") + jnp.dot(ya_ref[...], wo_ref[d_lru:, :], preferred_element_type=F32)
            state["x1"] = x_ref[...] + mix
            state["hm"] = _rms(state["x1"], gm_ref[...]).astype(BF16)
        elif t % 2 == 0:
            cs = slice((t - 2) // 2 * ff_chunk, ((t - 2) // 2 + 1) * ff_chunk)
            up = jnp.dot(state["hm"], wu_ref[:, cs], preferred_element_type=F32)
            state["act"] = jnp.square(jnp.maximum(up, 0.0)).astype(BF16)
        else:
            cs = slice((t - 3) // 2 * ff_chunk, ((t - 3) // 2 + 1) * ff_chunk)
            down = jnp.dot(state.pop("act"), wd_ref[cs, :], preferred_element_type=F32)
            state["mlp"] = down if t == 3 else state["mlp"] + down

    per_slot = -(-len(units) // (n_pieces - 2))
    pending = []
    for t in range(n_pieces):
        started = [(u, scores(*u)) for u in units[t * per_slot:(t + 1) * per_slot]]
        numer = [(u, softmax_numerator(*u, s)) for u, s in pending]
        mlp_piece(t)
        if 1 <= t <= n_blk:
            lru_finish(t - 1)
        for u, (e, e_sink) in numer:
            weighted_values(*u, e, e_sink)
        pending = started
    assert not pending

    x2 = state["x1"] + state["mlp"]
    o_ref[...] = _rms(x2, gf_ref[...]) if final_norm else x2
    ya_ref[...] = _rms(y_ref[...], og_ref[...]).astype(BF16)


def _attn_mlp(sinks, q2, kv2, attn_g, h2, gl2, lru_g, x2, w_out_bf, g_mlp, w_up_bf, w_down_bf, g_final,
              tiles_per_seq, final_norm):
    n, d = x2.shape
    d_lru, d_attn, d_kv2 = h2.shape[1], q2.shape[1], kv2.shape[1]
    d_ff = w_up_bf.shape[1]
    tm = TQ_ATTN
    n_tiles = n // tm
    bias = _attn_bias(d_attn // N_KV_HEADS // LANES * ATTN_BLOCK, ATTN_BLOCK)
    attn_tile = lambda s: (jnp.minimum(s, n_tiles - 1), 0)
    mlp_tile = lambda s: (jnp.maximum(s - 1, 0), 0)
    const = lambda s: (0, 0)
    wspec = lambda shape: pl.BlockSpec(shape, const, pipeline_mode=pl.Buffered(1))
    return pl.pallas_call(
        functools.partial(_attn_mlp_kernel, tiles_per_seq=tiles_per_seq, d_lru=d_lru, final_norm=final_norm),
        grid=(n_tiles + 1,),
        in_specs=[
            pl.BlockSpec(memory_space=pltpu.SMEM),
            pl.BlockSpec(bias.shape, lambda s: (0, 0, 0), pipeline_mode=pl.Buffered(1)),
            pl.BlockSpec((tm, d_attn), attn_tile),
            pl.BlockSpec((tm, d_kv2), attn_tile),
            pl.BlockSpec((1, d_attn), const),
            pl.BlockSpec((tm, d_lru), attn_tile),
            pl.BlockSpec((tm, d_lru), attn_tile),
            pl.BlockSpec((1, d_lru), const),
            pl.BlockSpec((tm, d), mlp_tile),
            wspec((d_lru + d_attn, d)),
            pl.BlockSpec((1, d), const),
            wspec((d, d_ff)),
            wspec((d_ff, d)),
            pl.BlockSpec((1, d), const),
        ],
        out_specs=pl.BlockSpec((tm, d), mlp_tile),
        out_shape=jax.ShapeDtypeStruct((n, d), F32),
        scratch_shapes=[
            pltpu.VMEM((ATTN_BLOCK, d_kv2), F32),
            pltpu.VMEM((tm, d_attn), F32),
            pltpu.VMEM((tm, d_attn), BF16),
            pltpu.VMEM((tm, d_lru), BF16),
        ],
        compiler_params=pltpu.CompilerParams(
            dimension_semantics=("arbitrary",), vmem_limit_bytes=VMEM_LIMIT),
        name="attn_mlp",
    )(sinks, bias, q2, kv2, attn_g, h2, gl2, lru_g, x2, w_out_bf, g_mlp, w_up_bf, w_down_bf, g_final)


def _pack_gates(w_a, w_x):
    def bd(w):
        nblk, bw, _ = w.shape
        per = GATE_CHUNK // bw
        w4 = w.reshape(nblk // per, per, bw, bw)
        eye = jnp.eye(per, dtype=w.dtype)
        return jnp.einsum('cikl,ij->cikjl', w4, eye).reshape(nblk // per, GATE_CHUNK, GATE_CHUNK)
    return jnp.concatenate([bd(w_a), bd(w_x)], axis=-1).astype(BF16)


def kernel(x, norm_mix_g, w_in, conv_w, conv_b, w_gate_a, b_gate_a, w_gate_x, b_gate_x, lru_lambda,
           attn_sinks, lru_out_g, attn_out_g, w_out, norm_mlp_g, w_mlp_up, w_mlp_down, norm_final_g):
    bsz, t, d = x.shape
    depth = w_in.shape[0]
    d_lru = conv_w.shape[2]
    d_attn = attn_out_g.shape[1]
    d_kv2 = w_in.shape[2] - 2 * d_lru - d_attn
    assert d_kv2 == 2 * LANES and d_attn == N_Q_HEADS * HEAD_DIM and d_lru % GATE_CHUNK == 0
    assert bsz == SUBLANES and ATTN_BLOCK == LANES
    assert t % (TM_PROJ // bsz) == 0 and t % TM_PROJ == 0
    assert t % TQ_ATTN == 0 and TQ_ATTN % ATTN_BLOCK == 0
    n = bsz * t
    x2 = x.reshape(n, d)
    r1 = lambda v: v.reshape(1, -1)
    for l in range(depth):
        h, gl, q, kv, w_out_bf, w_up_bf, w_down_bf = _in_lru(
            x2.reshape(bsz, t, d), r1(norm_mix_g[l]), w_in[l], conv_w[l], r1(conv_b[l]),
            _pack_gates(w_gate_a[l], w_gate_x[l]), r1(b_gate_a[l]), r1(b_gate_x[l]), r1(lru_lambda[l]),
            w_out[l], w_mlp_up[l], w_mlp_down[l], d_attn, d_kv2)
        x2 = _attn_mlp(attn_sinks[l], q.reshape(n, d_attn), kv.reshape(n, d_kv2), r1(attn_out_g[l]),
                       h.reshape(n, d_lru), gl.reshape(n, d_lru), r1(lru_out_g[l]), x2, w_out_bf,
                       r1(norm_mlp_g[l]), w_up_bf, w_down_bf, r1(norm_final_g),
                       tiles_per_seq=t // TQ_ATTN, final_norm=(l == depth - 1))
    return x2.reshape(bsz, t, d)
```

```python
import functools
import math

import jax
import jax.numpy as jnp
from jax import lax
from jax.experimental import pallas as pl
from jax.experimental.pallas import tpu as pltpu

F32 = jnp.float32
BF16 = jnp.bfloat16

EPS = 1e-6
NEG_INF = -1e30
LRU_C = 8.0
GELU_K1 = 2.0 * math.sqrt(2.0 / math.pi)
GELU_K3 = GELU_K1 * 0.044715
CONV_WIDTH = 4
HEAD_DIM = 64
N_Q_HEADS = 16
N_KV_HEADS = 2
ATTN_BLOCK = 128
LANES = 128
SUBLANES = 8
BF16_ROWS = 16
GATE_CHUNK = 256

TM_PROJ = 512
TQ_ATTN = 512
VMEM_LIMIT = 56 * 1024 * 1024


def _rms(x, g):
    ms = jnp.mean(x * x, axis=-1, keepdims=True)
    return x * lax.rsqrt(ms + EPS) * g


def _in_lru_kernel(x_ref, g_ref, w_ref, cw_ref, cb_ref, wg_ref, ba_ref, bx_ref, lam_ref, og_ref,
                   wo32_ref, wu32_ref, wd32_ref,
                   yl_ref, q_ref, kv_ref, wo_ref, wu_ref, wd_ref,
                   hn_ref, tail_ref, hc_ref, y_ref, ybt_ref, *, d_lru, d_attn):
    nb, tt, d = x_ref.shape
    r = nb * tt
    halo = (CONV_WIDTH - 1) * nb
    s1, s2, s3 = d_lru, 2 * d_lru, 2 * d_lru + d_attn

    @pl.when(pl.program_id(0) == 0)
    def _():
        tail_ref[...] = jnp.zeros_like(tail_ref)
        hc_ref[...] = jnp.zeros_like(hc_ref)

    wo_ref[...] = wo32_ref[...].astype(BF16)
    wu_ref[...] = wu32_ref[...].astype(BF16)
    wd_ref[...] = wd32_ref[...].astype(BF16)

    hn_bt = []
    for b in range(nb):
        hb = _rms(x_ref[b], g_ref[...])
        hn_bt.append(hb.astype(BF16))
        for s in range(d // LANES):
            hn_ref[s, pl.ds(b, tt, stride=nb), :] = hb[:, s * LANES:(s + 1) * LANES]
    hn_bt = jnp.concatenate(hn_bt, axis=0)
    hn = jnp.concatenate([hn_ref[s] for s in range(d // LANES)], axis=1).astype(BF16)

    def project(lo_col, width=GATE_CHUNK, lhs=None):
        return jnp.dot(hn if lhs is None else lhs, w_ref[:, lo_col:lo_col + width], preferred_element_type=F32)

    n_chunks = d_lru // GATE_CHUNK
    q_cols = d_attn // n_chunks
    nxt = (project(0), project(s1))
    for c in range(n_chunks):
        cs = slice(c * GATE_CHUNK, (c + 1) * GATE_CHUNK)
        x, gate = nxt
        xe = jnp.concatenate([tail_ref[:, cs], x], axis=0)
        tail_ref[:, cs] = x[r - halo:, :]
        xc = cb_ref[:, cs] + cw_ref[CONV_WIDTH - 1:CONV_WIDTH, cs] * x
        for k in range(CONV_WIDTH - 1):
            xc = xc + cw_ref[k:k + 1, cs] * xe[k * nb:k * nb + r]
        z = jnp.dot(xc.astype(BF16), wg_ref[c], preferred_element_type=F32)
        if c + 1 < n_chunks:
            nxt = (project((c + 1) * GATE_CHUNK), project(s1 + (c + 1) * GATE_CHUNK))
        q_c = project(s2 + c * q_cols, q_cols, hn_bt).astype(BF16)
        q_ref[:, :, c * q_cols:(c + 1) * q_cols] = q_c.reshape(nb, tt, q_cols)
        gate_r = jax.nn.sigmoid(z[:, :GATE_CHUNK] + ba_ref[:, cs])
        gate_i = jax.nn.sigmoid(z[:, GATE_CHUNK:] + bx_ref[:, cs])
        a = jnp.exp2(gate_r * ((-LRU_C * math.log2(math.e)) * jax.nn.softplus(-lam_ref[:, cs])))
        om = 1.0 - a * a
        bterm = jnp.where(om > 0.0, om * lax.rsqrt(om), 0.0) * (gate_i * xc)
        h = hc_ref[:, cs]
        hs = []
        for t in range(tt):
            h = a[t * nb:(t + 1) * nb] * h + bterm[t * nb:(t + 1) * nb]
            hs.append(h)
        hc_ref[:, cs] = h
        gelu = gate * jax.nn.sigmoid(gate * (GELU_K1 + GELU_K3 * (gate * gate)))
        y = jnp.concatenate(hs, axis=0) * gelu
        slabs = range(c * (GATE_CHUNK // LANES), (c + 1) * (GATE_CHUNK // LANES))
        for i, s in enumerate(slabs):
            y_ref[s] = y[:, i * LANES:(i + 1) * LANES]
        for b in range(nb):
            ybt_ref[b, :, cs] = jnp.concatenate([y_ref[s, pl.ds(b, tt, stride=nb), :] for s in slabs], axis=1)
    kv = project(s3, kv_ref.shape[2], hn_bt)
    kv_ref[...] = kv.reshape(nb, tt, kv.shape[1])
    for b in range(nb):
        yl_ref[b] = _rms(ybt_ref[b], og_ref[...]).astype(BF16)


def _in_lru(x, g, w_in, conv_w, conv_b, wg, b_a, b_x, lam, out_g, w_out, w_up, w_down, d_attn, d_kv2):
    bsz, t, d = x.shape
    e = w_in.shape[1]
    d_lru = conv_w.shape[1]
    tt = TM_PROJ // bsz
    r = tt * bsz
    n_steps = t // tt
    slab = lambda w: pl.BlockSpec((w.shape[0] // n_steps, w.shape[1]), lambda i: (i, 0))
    for w in (w_out, w_up, w_down):
        assert w.shape[0] % (n_steps * BF16_ROWS) == 0
    c2 = lambda i: (0, 0)
    tile = lambda i: (0, i, 0)
    vec = pl.BlockSpec((1, d_lru), c2)
    return pl.pallas_call(
        functools.partial(_in_lru_kernel, d_lru=d_lru, d_attn=d_attn),
        grid=(t // tt,),
        in_specs=[
            pl.BlockSpec((bsz, tt, d), tile),
            pl.BlockSpec((1, d), c2),
            pl.BlockSpec((d, e), c2, pipeline_mode=pl.Buffered(1)),
            pl.BlockSpec((CONV_WIDTH, d_lru), c2),
            vec,
            pl.BlockSpec(wg.shape, lambda i: (0, 0, 0)),
            vec, vec, vec, vec,
            slab(w_out), slab(w_up), slab(w_down),
        ],
        out_specs=[
            pl.BlockSpec((bsz, tt, d_lru), tile),
            pl.BlockSpec((bsz, tt, d_attn), tile),
            pl.BlockSpec((bsz, tt, d_kv2), tile),
            slab(w_out), slab(w_up), slab(w_down),
        ],
        out_shape=[
            jax.ShapeDtypeStruct((bsz, t, d_lru), BF16),
            jax.ShapeDtypeStruct((bsz, t, d_attn), BF16),
            jax.ShapeDtypeStruct((bsz, t, d_kv2), F32),
            jax.ShapeDtypeStruct(w_out.shape, BF16),
            jax.ShapeDtypeStruct(w_up.shape, BF16),
            jax.ShapeDtypeStruct(w_down.shape, BF16),
        ],
        scratch_shapes=[
            pltpu.VMEM((d // LANES, r, LANES), F32),
            pltpu.VMEM(((CONV_WIDTH - 1) * bsz, d_lru), F32),
            pltpu.VMEM((bsz, d_lru), F32),
            pltpu.VMEM((d_lru // LANES, r, LANES), F32),
            pltpu.VMEM((bsz, tt, d_lru), F32),
        ],
        compiler_params=pltpu.CompilerParams(
            dimension_semantics=("arbitrary",), vmem_limit_bytes=VMEM_LIMIT),
        name="in_lru",
    )(x, g, w_in, conv_w, conv_b, wg, b_a, b_x, lam, out_g, w_out, w_up, w_down)


def _attn_bias(tile_rows, blk):
    qi = lax.broadcasted_iota(jnp.int32, (tile_rows, 2 * blk), 0) % blk
    kj = lax.broadcasted_iota(jnp.int32, (tile_rows, 2 * blk), 1)
    band = (kj > qi) & (kj <= qi + blk)
    return jnp.stack([jnp.where(band, 0.0, NEG_INF), jnp.where(band & (kj >= blk), 0.0, NEG_INF)]).astype(F32)


def _attn_mlp_kernel(sink_ref, bias_ref, q_ref, kv_ref, og_ref, x_ref, yl_ref, wo_ref, gm_ref, wu_ref, wd_ref,
                     gf_ref, o_ref, kvprev_ref, y_ref, ya_ref, *, tiles_per_seq, d_lru, final_norm):
    tq, d_attn = q_ref.shape
    blk = ATTN_BLOCK
    n_blk = tq // blk
    step = pl.program_id(0)
    j = step % tiles_per_seq
    heads_per_tile = LANES // HEAD_DIM
    tiles_per_kv = d_attn // N_KV_HEADS // LANES

    @pl.when(step == 0)
    def _():
        ya_ref[...] = jnp.zeros_like(ya_ref)

    @pl.when(j == 0)
    def _():
        kvprev_ref[...] = jnp.zeros_like(kvprev_ref)

    kv_cur = kv_ref[...]
    kv_all = jnp.concatenate([kvprev_ref[...], kv_cur], axis=0)
    kvprev_ref[...] = kv_cur[tq - blk:, :]
    k_all = kv_all[:, :LANES] * (1.0 / math.sqrt(HEAD_DIM))
    v_all = kv_all[:, LANES:]
    k_rol = pltpu.roll(k_all, HEAD_DIM, axis=1)
    v_rol = pltpu.roll(v_all, HEAD_DIM, axis=1)
    lo = lax.broadcasted_iota(jnp.int32, k_all.shape, 1) < HEAD_DIM
    ones = jnp.ones(k_all.shape, BF16)

    def half(x_same, x_rolled, kv_head, parity):
        src = x_same if kv_head == parity else x_rolled
        keep = lo if parity == 0 else jnp.logical_not(lo)
        return jnp.where(keep, src, 0.0).astype(BF16)

    k_hp = [[half(k_all, k_rol, h, p) for p in range(heads_per_tile)] for h in range(N_KV_HEADS)]
    v_hp = [[jnp.concatenate([half(v_all, v_rol, h, p), ones], axis=1) for p in range(heads_per_tile)]
            for h in range(N_KV_HEADS)]

    units = [(i, h, p) for i in range(n_blk) for h in range(N_KV_HEADS) for p in range(heads_per_tile)]

    def scores(i, h, p):
        q_st = jnp.concatenate(
            [q_ref[i * blk:(i + 1) * blk, (h * tiles_per_kv + c) * LANES:(h * tiles_per_kv + c + 1) * LANES]
             for c in range(tiles_per_kv)], axis=0)
        bias = bias_ref[jnp.where(j == 0, 1, 0)] if i == 0 else bias_ref[0]
        return lax.dot_general(q_st, k_hp[h][p][i * blk:(i + 2) * blk], (((1,), (1,)), ((), ())),
                               preferred_element_type=F32) + bias

    def softmax_numerator(i, h, p, s):
        sink = jnp.concatenate(
            [jnp.full((blk, LANES), sink_ref[(h * tiles_per_kv + c) * heads_per_tile + p], F32)
             for c in range(tiles_per_kv)], axis=0)
        m = jnp.maximum(jnp.broadcast_to(jnp.max(s, axis=-1, keepdims=True), sink.shape), sink)
        e = jnp.concatenate([jnp.exp(s[:, :blk] - m), jnp.exp(s[:, blk:] - m)], axis=1).astype(BF16)
        return e, jnp.exp(sink - m)

    def weighted_values(i, h, p, e, e_sink):
        ov = jnp.dot(e, v_hp[h][p][i * blk:(i + 2) * blk], preferred_element_type=F32)
        o = ov[:, :LANES] / (ov[:, LANES:] + e_sink)
        for c in range(tiles_per_kv):
            col = (h * tiles_per_kv + c) * LANES
            o_c = o[c * blk:(c + 1) * blk, :]
            if p == 0:
                y_ref[i * blk:(i + 1) * blk, col:col + LANES] = o_c
            else:
                y_ref[i * blk:(i + 1) * blk, col:col + LANES] += o_c

    n_ff = 4
    ff_chunk = wu_ref.shape[1] // n_ff
    n_pieces = 2 + 2 * n_ff
    state = {}

    def mlp_piece(t):
        if t == 0:
            state["mix"] = jnp.dot(yl_ref[...], wo_ref[:d_lru, :], preferred_element_type=F32)
        elif t == 1:
            mix = state.pop("mix") + jnp.dot(ya_ref[...], wo_ref[d_lru:, :], preferred_element_type=F32)
            state["x1"] = x_ref[...] + mix
            state["hm"] = _rms(state["x1"], gm_ref[...]).astype(BF16)
        elif t % 2 == 0:
            cs = slice((t - 2) // 2 * ff_chunk, ((t - 2) // 2 + 1) * ff_chunk)
            up = jnp.dot(state["hm"], wu_ref[:, cs], preferred_element_type=F32)
            state["act"] = jnp.square(jnp.maximum(up, 0.0)).astype(BF16)
        else:
            cs = slice((t - 3) // 2 * ff_chunk, ((t - 3) // 2 + 1) * ff_chunk)
            down = jnp.dot(state.pop("act"), wd_ref[cs, :], preferred_element_type=F32)
            state["mlp"] = down if t == 3 else state["mlp"] + down

    per_slot = -(-len(units) // (n_pieces - 2))
    pending = []
    for t in range(n_pieces):
        started = [(u, scores(*u)) for u in units[t * per_slot:(t + 1) * per_slot]]
        numer = [(u, softmax_numerator(*u, s)) for u, s in pending]
        mlp_piece(t)
        for u, (e, e_sink) in numer:
            weighted_values(*u, e, e_sink)
        pending = started
    assert not pending

    x2 = state["x1"] + state["mlp"]
    o_ref[...] = _rms(x2, gf_ref[...]) if final_norm else x2
    ya_ref[...] = _rms(y_ref[...], og_ref[...]).astype(BF16)


def _attn_mlp(sinks, q2, kv2, attn_g, x2, yl2, w_out_bf, g_mlp, w_up_bf, w_down_bf, g_final, tiles_per_seq,
              final_norm):
    n, d = x2.shape
    d_lru, d_attn, d_kv2 = yl2.shape[1], q2.shape[1], kv2.shape[1]
    d_ff = w_up_bf.shape[1]
    tm = TQ_ATTN
    n_tiles = n // tm
    bias = _attn_bias(d_attn // N_KV_HEADS // LANES * ATTN_BLOCK, ATTN_BLOCK)
    attn_tile = lambda s: (jnp.minimum(s, n_tiles - 1), 0)
    mlp_tile = lambda s: (jnp.maximum(s - 1, 0), 0)
    const = lambda s: (0, 0)
    wspec = lambda shape: pl.BlockSpec(shape, const, pipeline_mode=pl.Buffered(1))
    return pl.pallas_call(
        functools.partial(_attn_mlp_kernel, tiles_per_seq=tiles_per_seq, d_lru=d_lru, final_norm=final_norm),
        grid=(n_tiles + 1,),
        in_specs=[
            pl.BlockSpec(memory_space=pltpu.SMEM),
            pl.BlockSpec(bias.shape, lambda s: (0, 0, 0), pipeline_mode=pl.Buffered(1)),
            pl.BlockSpec((tm, d_attn), attn_tile),
            pl.BlockSpec((tm, d_kv2), attn_tile),
            pl.BlockSpec((1, d_attn), const),
            pl.BlockSpec((tm, d), mlp_tile),
            pl.BlockSpec((tm, d_lru), mlp_tile),
            wspec((d_lru + d_attn, d)),
            pl.BlockSpec((1, d), const),
            wspec((d, d_ff)),
            wspec((d_ff, d)),
            pl.BlockSpec((1, d), const),
        ],
        out_specs=pl.BlockSpec((tm, d), mlp_tile),
        out_shape=jax.ShapeDtypeStruct((n, d), F32),
        scratch_shapes=[
            pltpu.VMEM((ATTN_BLOCK, d_kv2), F32),
            pltpu.VMEM((tm, d_attn), F32),
            pltpu.VMEM((tm, d_attn), BF16),
        ],
        compiler_params=pltpu.CompilerParams(
            dimension_semantics=("arbitrary",), vmem_limit_bytes=VMEM_LIMIT),
        name="attn_mlp",
    )(sinks, bias, q2, kv2, attn_g, x2, yl2, w_out_bf, g_mlp, w_up_bf, w_down_bf, g_final)


def _pack_gates(w_a, w_x):
    def bd(w):
        nblk, bw, _ = w.shape
        per = GATE_CHUNK // bw
        w4 = w.reshape(nblk // per, per, bw, bw)
        eye = jnp.eye(per, dtype=w.dtype)
        return jnp.einsum('cikl,ij->cikjl', w4, eye).reshape(nblk // per, GATE_CHUNK, GATE_CHUNK)
    return jnp.concatenate([bd(w_a), bd(w_x)], axis=-1).astype(BF16)


def kernel(x, norm_mix_g, w_in, conv_w, conv_b, w_gate_a, b_gate_a, w_gate_x, b_gate_x, lru_lambda,
           attn_sinks, lru_out_g, attn_out_g, w_out, norm_mlp_g, w_mlp_up, w_mlp_down, norm_final_g):
    bsz, t, d = x.shape
    depth = w_in.shape[0]
    d_lru = conv_w.shape[2]
    d_attn = attn_out_g.shape[1]
    d_kv2 = w_in.shape[2] - 2 * d_lru - d_attn
    assert d_kv2 == 2 * LANES and d_attn == N_Q_HEADS * HEAD_DIM and d_lru % GATE_CHUNK == 0
    assert bsz == SUBLANES and ATTN_BLOCK == LANES
    assert t % (TM_PROJ // bsz) == 0 and t % TM_PROJ == 0
    assert t % TQ_ATTN == 0 and TQ_ATTN % ATTN_BLOCK == 0
    n = bsz * t
    x2 = x.reshape(n, d)
    r1 = lambda v: v.reshape(1, -1)
    for l in range(depth):
        yl, q, kv, w_out_bf, w_up_bf, w_down_bf = _in_lru(
            x2.reshape(bsz, t, d), r1(norm_mix_g[l]), w_in[l].astype(BF16), conv_w[l], r1(conv_b[l]),
            _pack_gates(w_gate_a[l], w_gate_x[l]), r1(b_gate_a[l]), r1(b_gate_x[l]), r1(lru_lambda[l]),
            r1(lru_out_g[l]), w_out[l], w_mlp_up[l], w_mlp_down[l], d_attn, d_kv2)
        x2 = _attn_mlp(attn_sinks[l], q.reshape(n, d_attn), kv.reshape(n, d_kv2), r1(attn_out_g[l]), x2,
                       yl.reshape(n, d_lru), w_out_bf, r1(norm_mlp_g[l]), w_up_bf, w_down_bf, r1(norm_final_g),
                       tiles_per_seq=t // TQ_ATTN, final_norm=(l == depth - 1))
    return x2.reshape(bsz, t, d)
```

```python
import functools
import math

import jax
import jax.numpy as jnp
from jax import lax
from jax.experimental import pallas as pl
from jax.experimental.pallas import tpu as pltpu

F32 = jnp.float32
BF16 = jnp.bfloat16

EPS = 1e-6
NEG_INF = -1e30
LRU_C = 8.0
GELU_K1 = 2.0 * math.sqrt(2.0 / math.pi)
GELU_K3 = GELU_K1 * 0.044715
CONV_WIDTH = 4
HEAD_DIM = 64
N_Q_HEADS = 16
N_KV_HEADS = 2
ATTN_BLOCK = 128
LANES = 128
SUBLANES = 8
BF16_ROWS = 16
GATE_CHUNK = 256

TM_PROJ = 512
TQ_ATTN = 512
VMEM_LIMIT = 56 * 1024 * 1024


def _rms(x, g):
    ms = jnp.mean(x * x, axis=-1, keepdims=True)
    return x * lax.rsqrt(ms + EPS) * g


def _in_lru_kernel(x_ref, g_ref, w32_ref, cw_ref, cb_ref, wg_ref, ba_ref, bx_ref, lam_ref, og_ref,
                   wo32_ref, wu32_ref, wd32_ref,
                   yl_ref, q_ref, kv_ref, wo_ref, wu_ref, wd_ref,
                   w_ref, hn_ref, tail_ref, hc_ref, y_ref, *, d_lru, d_attn):
    nb, tt, d = x_ref.shape
    r = nb * tt
    halo = (CONV_WIDTH - 1) * nb
    s1, s2, s3 = d_lru, 2 * d_lru, 2 * d_lru + d_attn

    @pl.when(pl.program_id(0) == 0)
    def _():
        tail_ref[...] = jnp.zeros_like(tail_ref)
        hc_ref[...] = jnp.zeros_like(hc_ref)
        w_ref[...] = w32_ref[...].astype(BF16)

    wo_ref[...] = wo32_ref[...].astype(BF16)
    wu_ref[...] = wu32_ref[...].astype(BF16)
    wd_ref[...] = wd32_ref[...].astype(BF16)

    hn_bt = []
    for b in range(nb):
        hb = _rms(x_ref[b], g_ref[...])
        hn_bt.append(hb.astype(BF16))
        for s in range(d // LANES):
            hn_ref[s, pl.ds(b, tt, stride=nb), :] = hb[:, s * LANES:(s + 1) * LANES]
    hn_bt = jnp.concatenate(hn_bt, axis=0)
    hn = jnp.concatenate([hn_ref[s] for s in range(d // LANES)], axis=1).astype(BF16)

    def project(lo_col, width=GATE_CHUNK, lhs=None):
        return jnp.dot(hn if lhs is None else lhs, w_ref[:, lo_col:lo_col + width], preferred_element_type=F32)

    n_chunks = d_lru // GATE_CHUNK
    q_cols = d_attn // n_chunks
    ssq = jnp.zeros((r, 1), F32)
    nxt = (project(0), project(s1))
    for c in range(n_chunks):
        cs = slice(c * GATE_CHUNK, (c + 1) * GATE_CHUNK)
        x, gate = nxt
        xe = jnp.concatenate([tail_ref[:, cs], x], axis=0)
        tail_ref[:, cs] = x[r - halo:, :]
        xc = cb_ref[:, cs] + cw_ref[CONV_WIDTH - 1:CONV_WIDTH, cs] * x
        for k in range(CONV_WIDTH - 1):
            xc = xc + cw_ref[k:k + 1, cs] * xe[k * nb:k * nb + r]
        z = jnp.dot(xc.astype(BF16), wg_ref[c], preferred_element_type=F32)
        if c + 1 < n_chunks:
            nxt = (project((c + 1) * GATE_CHUNK), project(s1 + (c + 1) * GATE_CHUNK))
        q_c = project(s2 + c * q_cols, q_cols, hn_bt).astype(BF16)
        q_ref[:, :, c * q_cols:(c + 1) * q_cols] = q_c.reshape(nb, tt, q_cols)
        gate_r = jax.nn.sigmoid(z[:, :GATE_CHUNK] + ba_ref[:, cs])
        gate_i = jax.nn.sigmoid(z[:, GATE_CHUNK:] + bx_ref[:, cs])
        a = jnp.exp2(gate_r * ((-LRU_C * math.log2(math.e)) * jax.nn.softplus(-lam_ref[:, cs])))
        om = 1.0 - a * a
        bterm = jnp.where(om > 0.0, om * lax.rsqrt(om), 0.0) * (gate_i * xc)
        h = hc_ref[:, cs]
        hs = []
        for t in range(tt):
            h = a[t * nb:(t + 1) * nb] * h + bterm[t * nb:(t + 1) * nb]
            hs.append(h)
        hc_ref[:, cs] = h
        gelu = gate * jax.nn.sigmoid(gate * (GELU_K1 + GELU_K3 * (gate * gate)))
        y = jnp.concatenate(hs, axis=0) * gelu
        for s in range(GATE_CHUNK // LANES):
            y_ref[c * (GATE_CHUNK // LANES) + s] = y[:, s * LANES:(s + 1) * LANES]
        ssq = ssq + jnp.sum(y * y, axis=-1, keepdims=True)
    kv = project(s3, kv_ref.shape[2], hn_bt)
    kv_ref[...] = kv.reshape(nb, tt, kv.shape[1])
    scale = lax.rsqrt(ssq * (1.0 / d_lru) + EPS)
    n_slab = d_lru // LANES
    for s in range(n_slab):
        y_ref[s] = y_ref[s] * scale * og_ref[:, s * LANES:(s + 1) * LANES]
    for b in range(nb):
        yl_ref[b] = jnp.concatenate(
            [y_ref[s, pl.ds(b, tt, stride=nb), :] for s in range(n_slab)], axis=1).astype(BF16)


def _in_lru(x, g, w_in, conv_w, conv_b, wg, b_a, b_x, lam, out_g, w_out, w_up, w_down, d_attn, d_kv2):
    bsz, t, d = x.shape
    e = w_in.shape[1]
    d_lru = conv_w.shape[1]
    tt = TM_PROJ // bsz
    r = tt * bsz
    n_steps = t // tt
    slab = lambda w: pl.BlockSpec((w.shape[0] // n_steps, w.shape[1]), lambda i: (i, 0))
    for w in (w_out, w_up, w_down):
        assert w.shape[0] % (n_steps * BF16_ROWS) == 0
    c2 = lambda i: (0, 0)
    tile = lambda i: (0, i, 0)
    vec = pl.BlockSpec((1, d_lru), c2)
    return pl.pallas_call(
        functools.partial(_in_lru_kernel, d_lru=d_lru, d_attn=d_attn),
        grid=(t // tt,),
        in_specs=[
            pl.BlockSpec((bsz, tt, d), tile),
            pl.BlockSpec((1, d), c2),
            pl.BlockSpec((d, e), c2, pipeline_mode=pl.Buffered(1)),
            pl.BlockSpec((CONV_WIDTH, d_lru), c2),
            vec,
            pl.BlockSpec(wg.shape, lambda i: (0, 0, 0)),
            vec, vec, vec, vec,
            slab(w_out), slab(w_up), slab(w_down),
        ],
        out_specs=[
            pl.BlockSpec((bsz, tt, d_lru), tile),
            pl.BlockSpec((bsz, tt, d_attn), tile),
            pl.BlockSpec((bsz, tt, d_kv2), tile),
            slab(w_out), slab(w_up), slab(w_down),
        ],
        out_shape=[
            jax.ShapeDtypeStruct((bsz, t, d_lru), BF16),
            jax.ShapeDtypeStruct((bsz, t, d_attn), BF16),
            jax.ShapeDtypeStruct((bsz, t, d_kv2), F32),
            jax.ShapeDtypeStruct(w_out.shape, BF16),
            jax.ShapeDtypeStruct(w_up.shape, BF16),
            jax.ShapeDtypeStruct(w_down.shape, BF16),
        ],
        scratch_shapes=[
            pltpu.VMEM((d, e), BF16),
            pltpu.VMEM((d // LANES, r, LANES), F32),
            pltpu.VMEM(((CONV_WIDTH - 1) * bsz, d_lru), F32),
            pltpu.VMEM((bsz, d_lru), F32),
            pltpu.VMEM((d_lru // LANES, r, LANES), F32),
        ],
        compiler_params=pltpu.CompilerParams(
            dimension_semantics=("arbitrary",), vmem_limit_bytes=VMEM_LIMIT),
        name="in_lru",
    )(x, g, w_in, conv_w, conv_b, wg, b_a, b_x, lam, out_g, w_out, w_up, w_down)


def _attn_bias(tile_rows, blk):
    qi = lax.broadcasted_iota(jnp.int32, (tile_rows, 2 * blk), 0) % blk
    kj = lax.broadcasted_iota(jnp.int32, (tile_rows, 2 * blk), 1)
    band = (kj > qi) & (kj <= qi + blk)
    return jnp.stack([jnp.where(band, 0.0, NEG_INF), jnp.where(band & (kj >= blk), 0.0, NEG_INF)]).astype(F32)


def _attn_mlp_kernel(sink_ref, bias_ref, q_ref, kv_ref, og_ref, x_ref, yl_ref, wo_ref, gm_ref, wu_ref, wd_ref,
                     gf_ref, o_ref, kvprev_ref, y_ref, ya_ref, *, tiles_per_seq, d_lru, final_norm):
    tq, d_attn = q_ref.shape
    blk = ATTN_BLOCK
    n_blk = tq // blk
    step = pl.program_id(0)
    j = step % tiles_per_seq
    heads_per_tile = LANES // HEAD_DIM
    tiles_per_kv = d_attn // N_KV_HEADS // LANES

    @pl.when(step == 0)
    def _():
        ya_ref[...] = jnp.zeros_like(ya_ref)

    @pl.when(j == 0)
    def _():
        kvprev_ref[...] = jnp.zeros_like(kvprev_ref)

    kv_cur = kv_ref[...]
    kv_all = jnp.concatenate([kvprev_ref[...], kv_cur], axis=0)
    kvprev_ref[...] = kv_cur[tq - blk:, :]
    k_all = kv_all[:, :LANES] * (1.0 / math.sqrt(HEAD_DIM))
    v_all = kv_all[:, LANES:]
    k_rol = pltpu.roll(k_all, HEAD_DIM, axis=1)
    v_rol = pltpu.roll(v_all, HEAD_DIM, axis=1)
    lo = lax.broadcasted_iota(jnp.int32, k_all.shape, 1) < HEAD_DIM
    ones = jnp.ones(k_all.shape, BF16)

    def half(x_same, x_rolled, kv_head, parity):
        src = x_same if kv_head == parity else x_rolled
        keep = lo if parity == 0 else jnp.logical_not(lo)
        return jnp.where(keep, src, 0.0).astype(BF16)

    k_hp = [[half(k_all, k_rol, h, p) for p in range(heads_per_tile)] for h in range(N_KV_HEADS)]
    v_hp = [[jnp.concatenate([half(v_all, v_rol, h, p), ones], axis=1) for p in range(heads_per_tile)]
            for h in range(N_KV_HEADS)]

    units = [(i, h, p) for i in range(n_blk) for h in range(N_KV_HEADS) for p in range(heads_per_tile)]

    def scores(i, h, p):
        q_st = jnp.concatenate(
            [q_ref[i * blk:(i + 1) * blk, (h * tiles_per_kv + c) * LANES:(h * tiles_per_kv + c + 1) * LANES]
             for c in range(tiles_per_kv)], axis=0)
        bias = bias_ref[jnp.where(j == 0, 1, 0)] if i == 0 else bias_ref[0]
        return lax.dot_general(q_st, k_hp[h][p][i * blk:(i + 2) * blk], (((1,), (1,)), ((), ())),
                               preferred_element_type=F32) + bias

    def softmax_numerator(i, h, p, s):
        sink = jnp.concatenate(
            [jnp.full((blk, LANES), sink_ref[(h * tiles_per_kv + c) * heads_per_tile + p], F32)
             for c in range(tiles_per_kv)], axis=0)
        m = jnp.maximum(jnp.broadcast_to(jnp.max(s, axis=-1, keepdims=True), sink.shape), sink)
        e = jnp.concatenate([jnp.exp(s[:, :blk] - m), jnp.exp(s[:, blk:] - m)], axis=1).astype(BF16)
        return e, jnp.exp(sink - m)

    def weighted_values(i, h, p, e, e_sink):
        ov = jnp.dot(e, v_hp[h][p][i * blk:(i + 2) * blk], preferred_element_type=F32)
        o = ov[:, :LANES] / (ov[:, LANES:] + e_sink)
        for c in range(tiles_per_kv):
            col = (h * tiles_per_kv + c) * LANES
            o_c = o[c * blk:(c + 1) * blk, :]
            if p == 0:
                y_ref[i * blk:(i + 1) * blk, col:col + LANES] = o_c
            else:
                y_ref[i * blk:(i + 1) * blk, col:col + LANES] += o_c

    n_ff = 8
    ff_chunk = wu_ref.shape[1] // n_ff
    n_pieces = 2 + 2 * n_ff
    state = {}

    def mlp_piece(t):
        if t == 0:
            state["mix"] = jnp.dot(yl_ref[...], wo_ref[:d_lru, :], preferred_element_type=F32)
        elif t == 1:
            mix = state.pop("mix") + jnp.dot(ya_ref[...], wo_ref[d_lru:, :], preferred_element_type=F32)
            state["x1"] = x_ref[...] + mix
            state["hm"] = _rms(state["x1"], gm_ref[...]).astype(BF16)
        elif t % 2 == 0:
            cs = slice((t - 2) // 2 * ff_chunk, ((t - 2) // 2 + 1) * ff_chunk)
            up = jnp.dot(state["hm"], wu_ref[:, cs], preferred_element_type=F32)
            state["act"] = jnp.square(jnp.maximum(up, 0.0)).astype(BF16)
        else:
            cs = slice((t - 3) // 2 * ff_chunk, ((t - 3) // 2 + 1) * ff_chunk)
            down = jnp.dot(state.pop("act"), wd_ref[cs, :], preferred_element_type=F32)
            state["mlp"] = down if t == 3 else state["mlp"] + down

    per_slot = -(-len(units) // (n_pieces - 2))
    pending = []
    for t in range(n_pieces):
        started = [(u, scores(*u)) for u in units[t * per_slot:(t + 1) * per_slot]]
        numer = [(u, softmax_numerator(*u, s)) for u, s in pending]
        mlp_piece(t)
        for u, (e, e_sink) in numer:
            weighted_values(*u, e, e_sink)
        pending = started
    assert not pending

    x2 = state["x1"] + state["mlp"]
    o_ref[...] = _rms(x2, gf_ref[...]) if final_norm else x2
    ya_ref[...] = _rms(y_ref[...], og_ref[...]).astype(BF16)


def _attn_mlp(sinks, q2, kv2, attn_g, x2, yl2, w_out_bf, g_mlp, w_up_bf, w_down_bf, g_final, tiles_per_seq,
              final_norm):
    n, d = x2.shape
    d_lru, d_attn, d_kv2 = yl2.shape[1], q2.shape[1], kv2.shape[1]
    d_ff = w_up_bf.shape[1]
    tm = TQ_ATTN
    n_tiles = n // tm
    bias = _attn_bias(d_attn // N_KV_HEADS // LANES * ATTN_BLOCK, ATTN_BLOCK)
    attn_tile = lambda s: (jnp.minimum(s, n_tiles - 1), 0)
    mlp_tile = lambda s: (jnp.maximum(s - 1, 0), 0)
    const = lambda s: (0, 0)
    wspec = lambda shape: pl.BlockSpec(shape, const, pipeline_mode=pl.Buffered(1))
    return pl.pallas_call(
        functools.partial(_attn_mlp_kernel, tiles_per_seq=tiles_per_seq, d_lru=d_lru, final_norm=final_norm),
        grid=(n_tiles + 1,),
        in_specs=[
            pl.BlockSpec(memory_space=pltpu.SMEM),
            pl.BlockSpec(bias.shape, lambda s: (0, 0, 0), pipeline_mode=pl.Buffered(1)),
            pl.BlockSpec((tm, d_attn), attn_tile),
            pl.BlockSpec((tm, d_kv2), attn_tile),
            pl.BlockSpec((1, d_attn), const),
            pl.BlockSpec((tm, d), mlp_tile),
            pl.BlockSpec((tm, d_lru), mlp_tile),
            wspec((d_lru + d_attn, d)),
            pl.BlockSpec((1, d), const),
            wspec((d, d_ff)),
            wspec((d_ff, d)),
            pl.BlockSpec((1, d), const),
        ],
        out_specs=pl.BlockSpec((tm, d), mlp_tile),
        out_shape=jax.ShapeDtypeStruct((n, d), F32),
        scratch_shapes=[
            pltpu.VMEM((ATTN_BLOCK, d_kv2), F32),
            pltpu.VMEM((tm, d_attn), F32),
            pltpu.VMEM((tm, d_attn), BF16),
        ],
        compiler_params=pltpu.CompilerParams(
            dimension_semantics=("arbitrary",), vmem_limit_bytes=VMEM_LIMIT),
        name="attn_mlp",
    )(sinks, bias, q2, kv2, attn_g, x2, yl2, w_out_bf, g_mlp, w_up_bf, w_down_bf, g_final)


def _pack_gates(w_a, w_x):
    def bd(w):
        nblk, bw, _ = w.shape
        per = GATE_CHUNK // bw
        w4 = w.reshape(nblk // per, per, bw, bw)
        eye = jnp.eye(per, dtype=w.dtype)
        return jnp.einsum('cikl,ij->cikjl', w4, eye).reshape(nblk // per, GATE_CHUNK, GATE_CHUNK)
    return jnp.concatenate([bd(w_a), bd(w_x)], axis=-1).astype(BF16)


def kernel(x, norm_mix_g, w_in, conv_w, conv_b, w_gate_a, b_gate_a, w_gate_x, b_gate_x, lru_lambda,
           attn_sinks, lru_out_g, attn_out_g, w_out, norm_mlp_g, w_mlp_up, w_mlp_down, norm_final_g):
    bsz, t, d = x.shape
    depth = w_in.shape[0]
    d_lru = conv_w.shape[2]
    d_attn = attn_out_g.shape[1]
    d_kv2 = w_in.shape[2] - 2 * d_lru - d_attn
    assert d_kv2 == 2 * LANES and d_attn == N_Q_HEADS * HEAD_DIM and d_lru % GATE_CHUNK == 0
    assert bsz == SUBLANES and ATTN_BLOCK == LANES
    assert t % (TM_PROJ // bsz) == 0 and t % TM_PROJ == 0
    assert t % TQ_ATTN == 0 and TQ_ATTN % ATTN_BLOCK == 0
    n = bsz * t
    x2 = x.reshape(n, d)
    r1 = lambda v: v.reshape(1, -1)
    for l in range(depth):
        yl, q, kv, w_out_bf, w_up_bf, w_down_bf = _in_lru(
            x2.reshape(bsz, t, d), r1(norm_mix_g[l]), w_in[l], conv_w[l], r1(conv_b[l]),
            _pack_gates(w_gate_a[l], w_gate_x[l]), r1(b_gate_a[l]), r1(b_gate_x[l]), r1(lru_lambda[l]),
            r1(lru_out_g[l]), w_out[l], w_mlp_up[l], w_mlp_down[l], d_attn, d_kv2)
        x2 = _attn_mlp(attn_sinks[l], q.reshape(n, d_attn), kv.reshape(n, d_kv2), r1(attn_out_g[l]), x2,
                       yl.reshape(n, d_lru), w_out_bf, r1(norm_mlp_g[l]), w_up_bf, w_down_bf, r1(norm_final_g),
                       tiles_per_seq=t // TQ_ATTN, final_norm=(l == depth - 1))
    return x2.reshape(bsz, t, d)
```

```python
import functools
import math

import jax
import jax.numpy as jnp
from jax import lax
from jax.experimental import pallas as pl
from jax.experimental.pallas import tpu as pltpu

F32 = jnp.float32
BF16 = jnp.bfloat16

EPS = 1e-6
NEG_INF = -1e30
LRU_C = 8.0
GELU_K1 = 2.0 * math.sqrt(2.0 / math.pi)
GELU_K3 = GELU_K1 * 0.044715
CONV_WIDTH = 4
HEAD_DIM = 64
N_Q_HEADS = 16
N_KV_HEADS = 2
ATTN_BLOCK = 128
LANES = 128
SUBLANES = 8
BF16_ROWS = 16
GATE_CHUNK = 256

TM_PROJ = 512
TQ_ATTN = 512
VMEM_LIMIT = 56 * 1024 * 1024


def _rms(x, g):
    ms = jnp.mean(x * x, axis=-1, keepdims=True)
    return x * lax.rsqrt(ms + EPS) * g


def _in_lru_kernel(x_ref, g_ref, w32_ref, cw_ref, cb_ref, wg_ref, ba_ref, bx_ref, lam_ref, og_ref,
                   wo32_ref, wu32_ref, wd32_ref,
                   yl_ref, q_ref, kv_ref, wo_ref, wu_ref, wd_ref,
                   w_ref, hn_ref, tail_ref, hc_ref, y_ref, *, d_lru, d_attn):
    nb, tt, d = x_ref.shape
    r = nb * tt
    halo = (CONV_WIDTH - 1) * nb
    s1, s2, s3 = d_lru, 2 * d_lru, 2 * d_lru + d_attn

    @pl.when(pl.program_id(0) == 0)
    def _():
        tail_ref[...] = jnp.zeros_like(tail_ref)
        hc_ref[...] = jnp.zeros_like(hc_ref)
        w_ref[...] = w32_ref[...].astype(BF16)

    wo_ref[...] = wo32_ref[...].astype(BF16)
    wu_ref[...] = wu32_ref[...].astype(BF16)
    wd_ref[...] = wd32_ref[...].astype(BF16)

    hn_bt = []
    for b in range(nb):
        hb = _rms(x_ref[b], g_ref[...])
        hn_bt.append(hb.astype(BF16))
        for s in range(d // LANES):
            hn_ref[s, pl.ds(b, tt, stride=nb), :] = hb[:, s * LANES:(s + 1) * LANES]
    hn_bt = jnp.concatenate(hn_bt, axis=0)
    hn = jnp.concatenate([hn_ref[s] for s in range(d // LANES)], axis=1).astype(BF16)

    def project(lo_col, width=GATE_CHUNK, lhs=None):
        return jnp.dot(hn if lhs is None else lhs, w_ref[:, lo_col:lo_col + width], preferred_element_type=F32)

    n_chunks = d_lru // GATE_CHUNK
    q_cols = d_attn // n_chunks
    ssq = jnp.zeros((r, 1), F32)
    nxt = (project(0), project(s1))
    for c in range(n_chunks):
        cs = slice(c * GATE_CHUNK, (c + 1) * GATE_CHUNK)
        x, gate = nxt
        xe = jnp.concatenate([tail_ref[:, cs], x], axis=0)
        tail_ref[:, cs] = x[r - halo:, :]
        xc = cb_ref[:, cs] + cw_ref[CONV_WIDTH - 1:CONV_WIDTH, cs] * x
        for k in range(CONV_WIDTH - 1):
            xc = xc + cw_ref[k:k + 1, cs] * xe[k * nb:k * nb + r]
        z = jnp.dot(xc.astype(BF16), wg_ref[c], preferred_element_type=F32)
        if c + 1 < n_chunks:
            nxt = (project((c + 1) * GATE_CHUNK), project(s1 + (c + 1) * GATE_CHUNK))
        q_c = project(s2 + c * q_cols, q_cols, hn_bt).astype(BF16)
        q_ref[:, :, c * q_cols:(c + 1) * q_cols] = q_c.reshape(nb, tt, q_cols)
        gate_r = jax.nn.sigmoid(z[:, :GATE_CHUNK] + ba_ref[:, cs])
        gate_i = jax.nn.sigmoid(z[:, GATE_CHUNK:] + bx_ref[:, cs])
        a = jnp.exp2(gate_r * ((-LRU_C * math.log2(math.e)) * jax.nn.softplus(-lam_ref[:, cs])))
        om = 1.0 - a * a
        bterm = jnp.where(om > 0.0, om * lax.rsqrt(om), 0.0) * (gate_i * xc)
        h = hc_ref[:, cs]
        hs = []
        for t in range(tt):
            h = a[t * nb:(t + 1) * nb] * h + bterm[t * nb:(t + 1) * nb]
            hs.append(h)
        hc_ref[:, cs] = h
        gelu = gate * jax.nn.sigmoid(gate * (GELU_K1 + GELU_K3 * (gate * gate)))
        y = jnp.concatenate(hs, axis=0) * gelu
        for s in range(GATE_CHUNK // LANES):
            y_ref[c * (GATE_CHUNK // LANES) + s] = y[:, s * LANES:(s + 1) * LANES]
        ssq = ssq + jnp.sum(y * y, axis=-1, keepdims=True)
    kv = project(s3, kv_ref.shape[2], hn_bt)
    kv_ref[...] = kv.reshape(nb, tt, kv.shape[1])
    scale = lax.rsqrt(ssq * (1.0 / d_lru) + EPS)
    n_slab = d_lru // LANES
    for s in range(n_slab):
        y_ref[s] = y_ref[s] * scale * og_ref[:, s * LANES:(s + 1) * LANES]
    for b in range(nb):
        yl_ref[b] = jnp.concatenate(
            [y_ref[s, pl.ds(b, tt, stride=nb), :] for s in range(n_slab)], axis=1).astype(BF16)


def _in_lru(x, g, w_in, conv_w, conv_b, wg, b_a, b_x, lam, out_g, w_out, w_up, w_down, d_attn, d_kv2):
    bsz, t, d = x.shape
    e = w_in.shape[1]
    d_lru = conv_w.shape[1]
    tt = TM_PROJ // bsz
    r = tt * bsz
    n_steps = t // tt
    slab = lambda w: pl.BlockSpec((w.shape[0] // n_steps, w.shape[1]), lambda i: (i, 0))
    for w in (w_out, w_up, w_down):
        assert w.shape[0] % (n_steps * BF16_ROWS) == 0
    c2 = lambda i: (0, 0)
    tile = lambda i: (0, i, 0)
    vec = pl.BlockSpec((1, d_lru), c2)
    return pl.pallas_call(
        functools.partial(_in_lru_kernel, d_lru=d_lru, d_attn=d_attn),
        grid=(t // tt,),
        in_specs=[
            pl.BlockSpec((bsz, tt, d), tile),
            pl.BlockSpec((1, d), c2),
            pl.BlockSpec((d, e), c2, pipeline_mode=pl.Buffered(1)),
            pl.BlockSpec((CONV_WIDTH, d_lru), c2),
            vec,
            pl.BlockSpec(wg.shape, lambda i: (0, 0, 0)),
            vec, vec, vec, vec,
            slab(w_out), slab(w_up), slab(w_down),
        ],
        out_specs=[
            pl.BlockSpec((bsz, tt, d_lru), tile),
            pl.BlockSpec((bsz, tt, d_attn), tile),
            pl.BlockSpec((bsz, tt, d_kv2), tile),
            slab(w_out), slab(w_up), slab(w_down),
        ],
        out_shape=[
            jax.ShapeDtypeStruct((bsz, t, d_lru), BF16),
            jax.ShapeDtypeStruct((bsz, t, d_attn), BF16),
            jax.ShapeDtypeStruct((bsz, t, d_kv2), F32),
            jax.ShapeDtypeStruct(w_out.shape, BF16),
            jax.ShapeDtypeStruct(w_up.shape, BF16),
            jax.ShapeDtypeStruct(w_down.shape, BF16),
        ],
        scratch_shapes=[
            pltpu.VMEM((d, e), BF16),
            pltpu.VMEM((d // LANES, r, LANES), F32),
            pltpu.VMEM(((CONV_WIDTH - 1) * bsz, d_lru), F32),
            pltpu.VMEM((bsz, d_lru), F32),
            pltpu.VMEM((d_lru // LANES, r, LANES), F32),
        ],
        compiler_params=pltpu.CompilerParams(
            dimension_semantics=("arbitrary",), vmem_limit_bytes=VMEM_LIMIT),
        name="in_lru",
    )(x, g, w_in, conv_w, conv_b, wg, b_a, b_x, lam, out_g, w_out, w_up, w_down)


def _attn_bias(tile_rows, blk):
    qi = lax.broadcasted_iota(jnp.int32, (tile_rows, 2 * blk), 0) % blk
    kj = lax.broadcasted_iota(jnp.int32, (tile_rows, 2 * blk), 1)
    band = (kj > qi) & (kj <= qi + blk)
    return jnp.stack([jnp.where(band, 0.0, NEG_INF), jnp.where(band & (kj >= blk), 0.0, NEG_INF)]).astype(F32)


def _attn_mlp_kernel(sink_ref, bias_ref, q_ref, kv_ref, og_ref, x_ref, yl_ref, wo_ref, gm_ref, wu_ref, wd_ref,
                     gf_ref, o_ref, kvprev_ref, y_ref, ya_ref, *, tiles_per_seq, d_lru, final_norm):
    tq, d_attn = q_ref.shape
    blk = ATTN_BLOCK
    n_blk = tq // blk
    step = pl.program_id(0)
    j = step % tiles_per_seq
    heads_per_tile = LANES // HEAD_DIM
    tiles_per_kv = d_attn // N_KV_HEADS // LANES

    @pl.when(j == 0)
    def _():
        kvprev_ref[...] = jnp.zeros_like(kvprev_ref)

    kv_cur = kv_ref[...]
    kv_all = jnp.concatenate([kvprev_ref[...], kv_cur], axis=0)
    kvprev_ref[...] = kv_cur[tq - blk:, :]
    k_all = kv_all[:, :LANES] * (1.0 / math.sqrt(HEAD_DIM))
    v_all = kv_all[:, LANES:]
    k_rol = pltpu.roll(k_all, HEAD_DIM, axis=1)
    v_rol = pltpu.roll(v_all, HEAD_DIM, axis=1)
    lo = lax.broadcasted_iota(jnp.int32, k_all.shape, 1) < HEAD_DIM
    ones = jnp.ones(k_all.shape, BF16)

    def half(x_same, x_rolled, kv_head, parity):
        src = x_same if kv_head == parity else x_rolled
        keep = lo if parity == 0 else jnp.logical_not(lo)
        return jnp.where(keep, src, 0.0).astype(BF16)

    k_hp = [[half(k_all, k_rol, h, p) for p in range(heads_per_tile)] for h in range(N_KV_HEADS)]
    v_hp = [[jnp.concatenate([half(v_all, v_rol, h, p), ones], axis=1) for p in range(heads_per_tile)]
            for h in range(N_KV_HEADS)]

    units = [(i, h, p) for i in range(n_blk) for h in range(N_KV_HEADS) for p in range(heads_per_tile)]

    def scores(i, h, p):
        q_st = jnp.concatenate(
            [q_ref[i * blk:(i + 1) * blk, (h * tiles_per_kv + c) * LANES:(h * tiles_per_kv + c + 1) * LANES]
             for c in range(tiles_per_kv)], axis=0)
        bias = bias_ref[jnp.where(j == 0, 1, 0)] if i == 0 else bias_ref[0]
        return lax.dot_general(q_st, k_hp[h][p][i * blk:(i + 2) * blk], (((1,), (1,)), ((), ())),
                               preferred_element_type=F32) + bias

    def softmax_numerator(i, h, p, s):
        sink = jnp.concatenate(
            [jnp.full((blk, LANES), sink_ref[(h * tiles_per_kv + c) * heads_per_tile + p], F32)
             for c in range(tiles_per_kv)], axis=0)
        m = jnp.maximum(jnp.broadcast_to(jnp.max(s, axis=-1, keepdims=True), sink.shape), sink)
        e = jnp.concatenate([jnp.exp(s[:, :blk] - m), jnp.exp(s[:, blk:] - m)], axis=1).astype(BF16)
        return e, jnp.exp(sink - m)

    def weighted_values(i, h, p, e, e_sink):
        ov = jnp.dot(e, v_hp[h][p][i * blk:(i + 2) * blk], preferred_element_type=F32)
        o = ov[:, :LANES] / (ov[:, LANES:] + e_sink)
        for c in range(tiles_per_kv):
            col = (h * tiles_per_kv + c) * LANES
            o_c = o[c * blk:(c + 1) * blk, :]
            if p == 0:
                y_ref[i * blk:(i + 1) * blk, col:col + LANES] = o_c
            else:
                y_ref[i * blk:(i + 1) * blk, col:col + LANES] += o_c

    n_ff = 8
    ff_chunk = wu_ref.shape[1] // n_ff
    n_pieces = 2 + 2 * n_ff
    state = {}

    def mlp_piece(t):
        if t == 0:
            state["mix"] = jnp.dot(yl_ref[...], wo_ref[:d_lru, :], preferred_element_type=F32)
        elif t == 1:
            mix = state.pop("mix") + jnp.dot(ya_ref[...], wo_ref[d_lru:, :], preferred_element_type=F32)
            state["x1"] = x_ref[...] + mix
            state["hm"] = _rms(state["x1"], gm_ref[...]).astype(BF16)
        elif t % 2 == 0:
            cs = slice((t - 2) // 2 * ff_chunk, ((t - 2) // 2 + 1) * ff_chunk)
            up = jnp.dot(state["hm"], wu_ref[:, cs], preferred_element_type=F32)
            state["act"] = jnp.square(jnp.maximum(up, 0.0)).astype(BF16)
        else:
            cs = slice((t - 3) // 2 * ff_chunk, ((t - 3) // 2 + 1) * ff_chunk)
            down = jnp.dot(state.pop("act"), wd_ref[cs, :], preferred_element_type=F32)
            state["mlp"] = down if t == 3 else state["mlp"] + down

    per_slot = -(-len(units) // (n_pieces - 2))

    def pipeline(with_attn, with_mlp):
        state.clear()
        todo = units if with_attn else []
        pending = []
        for t in range(n_pieces):
            started = [(u, scores(*u)) for u in todo[t * per_slot:(t + 1) * per_slot]]
            numer = [(u, softmax_numerator(*u, s)) for u, s in pending]
            if with_mlp:
                mlp_piece(t)
            for u, (e, e_sink) in numer:
                weighted_values(*u, e, e_sink)
            pending = started
        assert not pending
        if with_mlp:
            x2 = state["x1"] + state["mlp"]
            o_ref[...] = _rms(x2, gf_ref[...]) if final_norm else x2
        if with_attn:
            ya_ref[...] = _rms(y_ref[...], og_ref[...]).astype(BF16)

    last = pl.num_programs(0) - 1
    lax.cond(step == 0,
             lambda: pipeline(True, False),
             lambda: lax.cond(step == last, lambda: pipeline(False, True), lambda: pipeline(True, True)))


def _attn_mlp(sinks, q2, kv2, attn_g, x2, yl2, w_out_bf, g_mlp, w_up_bf, w_down_bf, g_final, tiles_per_seq,
              final_norm):
    n, d = x2.shape
    d_lru, d_attn, d_kv2 = yl2.shape[1], q2.shape[1], kv2.shape[1]
    d_ff = w_up_bf.shape[1]
    tm = TQ_ATTN
    n_tiles = n // tm
    bias = _attn_bias(d_attn // N_KV_HEADS // LANES * ATTN_BLOCK, ATTN_BLOCK)
    attn_tile = lambda s: (jnp.minimum(s, n_tiles - 1), 0)
    mlp_tile = lambda s: (jnp.maximum(s - 1, 0), 0)
    const = lambda s: (0, 0)
    wspec = lambda shape: pl.BlockSpec(shape, const, pipeline_mode=pl.Buffered(1))
    return pl.pallas_call(
        functools.partial(_attn_mlp_kernel, tiles_per_seq=tiles_per_seq, d_lru=d_lru, final_norm=final_norm),
        grid=(n_tiles + 1,),
        in_specs=[
            pl.BlockSpec(memory_space=pltpu.SMEM),
            pl.BlockSpec(bias.shape, lambda s: (0, 0, 0), pipeline_mode=pl.Buffered(1)),
            pl.BlockSpec((tm, d_attn), attn_tile),
            pl.BlockSpec((tm, d_kv2), attn_tile),
            pl.BlockSpec((1, d_attn), const),
            pl.BlockSpec((tm, d), mlp_tile),
            pl.BlockSpec((tm, d_lru), mlp_tile),
            wspec((d_lru + d_attn, d)),
            pl.BlockSpec((1, d), const),
            wspec((d, d_ff)),
            wspec((d_ff, d)),
            pl.BlockSpec((1, d), const),
        ],
        out_specs=pl.BlockSpec((tm, d), mlp_tile),
        out_shape=jax.ShapeDtypeStruct((n, d), F32),
        scratch_shapes=[
            pltpu.VMEM((ATTN_BLOCK, d_kv2), F32),
            pltpu.VMEM((tm, d_attn), F32),
            pltpu.VMEM((tm, d_attn), BF16),
        ],
        compiler_params=pltpu.CompilerParams(
            dimension_semantics=("arbitrary",), vmem_limit_bytes=VMEM_LIMIT),
        name="attn_mlp",
    )(sinks, bias, q2, kv2, attn_g, x2, yl2, w_out_bf, g_mlp, w_up_bf, w_down_bf, g_final)


def _pack_gates(w_a, w_x):
    def bd(w):
        nblk, bw, _ = w.shape
        per = GATE_CHUNK // bw
        w4 = w.reshape(nblk // per, per, bw, bw)
        eye = jnp.eye(per, dtype=w.dtype)
        return jnp.einsum('cikl,ij->cikjl', w4, eye).reshape(nblk // per, GATE_CHUNK, GATE_CHUNK)
    return jnp.concatenate([bd(w_a), bd(w_x)], axis=-1).astype(BF16)


def kernel(x, norm_mix_g, w_in, conv_w, conv_b, w_gate_a, b_gate_a, w_gate_x, b_gate_x, lru_lambda,
           attn_sinks, lru_out_g, attn_out_g, w_out, norm_mlp_g, w_mlp_up, w_mlp_down, norm_final_g):
    bsz, t, d = x.shape
    depth = w_in.shape[0]
    d_lru = conv_w.shape[2]
    d_attn = attn_out_g.shape[1]
    d_kv2 = w_in.shape[2] - 2 * d_lru - d_attn
    assert d_kv2 == 2 * LANES and d_attn == N_Q_HEADS * HEAD_DIM and d_lru % GATE_CHUNK == 0
    assert bsz == SUBLANES and ATTN_BLOCK == LANES
    assert t % (TM_PROJ // bsz) == 0 and t % TM_PROJ == 0
    assert t % TQ_ATTN == 0 and TQ_ATTN % ATTN_BLOCK == 0
    n = bsz * t
    x2 = x.reshape(n, d)
    r1 = lambda v: v.reshape(1, -1)
    for l in range(depth):
        yl, q, kv, w_out_bf, w_up_bf, w_down_bf = _in_lru(
            x2.reshape(bsz, t, d), r1(norm_mix_g[l]), w_in[l], conv_w[l], r1(conv_b[l]),
            _pack_gates(w_gate_a[l], w_gate_x[l]), r1(b_gate_a[l]), r1(b_gate_x[l]), r1(lru_lambda[l]),
            r1(lru_out_g[l]), w_out[l], w_mlp_up[l], w_mlp_down[l], d_attn, d_kv2)
        x2 = _attn_mlp(attn_sinks[l], q.reshape(n, d_attn), kv.reshape(n, d_kv2), r1(attn_out_g[l]), x2,
                       yl.reshape(n, d_lru), w_out_bf, r1(norm_mlp_g[l]), w_up_bf, w_down_bf, r1(norm_final_g),
                       tiles_per_seq=t // TQ_ATTN, final_norm=(l == depth - 1))
    return x2.reshape(bsz, t, d)
```
